```python
import math
import jax, jax.numpy as jnp
from jax import lax
import numpy as np

D_MODEL = 1024
BATCH = 2
SEQ = 16384
DEPTH = 1
DEC_BATCH = 8
DEC_SEQ = 4096
PAST_LEN = 128

MIX_WIDTH = D_MODEL
ATT_HEADS = 8
ATT_HEAD_DIM = 64
ATT_WIDTH = ATT_HEADS * ATT_HEAD_DIM
DILATED_BRANCHES = ((128, 1), (512, 4), (2048, 16))
ATT_BLOCK = 64
ROT_DIM = ATT_HEAD_DIM // 4
ROPE_THETA = 500000.0
HG_HEADS = 4
HG_DK = 128
HG_DV = 128
HG_WIDTH = HG_HEADS * HG_DV
HG_CHUNK = 64
IN_WIDTH = 3 * ATT_WIDTH + 5 * HG_WIDTH
D_FF = ((8 * D_MODEL // 3 + 255) // 256) * 256
NORM_EPS = 1e-6
NEG_FILL = -1e30

kernel_name = "hybrid_dilated_attn_hgrn2_encoder"


def rmsnorm(x, g):
    x32 = x.astype(jnp.float32)
    y = x32 * lax.rsqrt(jnp.mean(x32 * x32, axis=-1, keepdims=True) + NORM_EPS)
    return (y * g.astype(jnp.float32)).astype(x.dtype)


def partial_rope(t, pos):
    t32 = t.astype(jnp.float32)
    inv_freq = ROPE_THETA ** (-jnp.arange(0, ROT_DIM, 2, dtype=jnp.float32) / ROT_DIM)
    ang = pos[:, None] * inv_freq[None, :]
    cos = jnp.cos(ang)[None, :, None, :]
    sin = jnp.sin(ang)[None, :, None, :]
    half = ROT_DIM // 2
    x1 = t32[..., :half]
    x2 = t32[..., half:ROT_DIM]
    rot = jnp.concatenate([x1 * cos - x2 * sin, x2 * cos + x1 * sin, t32[..., ROT_DIM:]], axis=-1)
    return rot


def dilated_branch(q, k, v, window, dil):
    B, S, H, D = q.shape
    L = S // dil
    half = window // (2 * dil)
    assert half <= ATT_BLOCK
    nb = -(-L // ATT_BLOCK)
    Lp = nb * ATT_BLOCK

    def to_sub(t):
        return t.reshape(B, L, dil, H, D).transpose(0, 2, 1, 3, 4)

    qs = jnp.pad(to_sub(q), ((0, 0), (0, 0), (0, Lp - L), (0, 0), (0, 0)))
    qs = qs.reshape(B, dil, nb, ATT_BLOCK, H, D)

    def band(t):
        tp = jnp.pad(to_sub(t), ((0, 0), (0, 0), (ATT_BLOCK, Lp - L + ATT_BLOCK), (0, 0), (0, 0)))
        tp = tp.reshape(B, dil, nb + 2, ATT_BLOCK, H, D)
        return jnp.concatenate([tp[:, :, :-2], tp[:, :, 1:-1], tp[:, :, 2:]], axis=3)

    kb = band(k)
    vb = band(v)
    qi = jnp.arange(ATT_BLOCK)[:, None]
    kj = jnp.arange(3 * ATT_BLOCK)[None, :]
    in_band = jnp.abs(kj - ATT_BLOCK - qi) <= half
    j_abs = jnp.arange(nb)[:, None, None] * ATT_BLOCK - ATT_BLOCK + kj[None]
    mask = in_band[None] & (j_abs >= 0) & (j_abs < L)

    s = jnp.einsum('bgnqhd,bgnkhd->bgnhqk', qs, kb) * (D ** -0.5)
    s = jnp.where(mask[None, None, :, None], s, NEG_FILL)
    m = jnp.max(s, axis=-1, keepdims=True)
    p = jnp.exp(s - m)
    l = jnp.sum(p, axis=-1)
    o = jnp.einsum('bgnhqk,bgnkhd->bgnqhd', p, vb) / l.transpose(0, 1, 2, 4, 3)[..., None]
    lse = (m[..., 0] + jnp.log(l)).transpose(0, 1, 2, 4, 3)
    o = o.reshape(B, dil, Lp, H, D)[:, :, :L].transpose(0, 2, 1, 3, 4).reshape(B, S, H, D)
    lse = lse.reshape(B, dil, Lp, H)[:, :, :L].transpose(0, 2, 1, 3).reshape(B, S, H)
    return o, lse


def dilated_attention(q, k, v):
    outs = []
    lses = []
    for window, dil in DILATED_BRANCHES:
        o, lse = dilated_branch(q, k, v, window, dil)
        outs.append(o)
        lses.append(lse)
    w = jax.nn.softmax(jnp.stack(lses, axis=0), axis=0)
    out = w[0][..., None] * outs[0]
    for b in range(1, len(outs)):
        out = out + w[b][..., None] * outs[b]
    return out


def hgrn2_chunk_scan(q, k, v, logf):
    B, S, H, DK = q.shape
    DV = v.shape[-1]
    N = S // HG_CHUNK

    def chunks(t):
        return t.reshape(B, N, HG_CHUNK, H, t.shape[-1]).transpose(1, 0, 3, 2, 4)

    tril = jnp.tril(jnp.ones((HG_CHUNK, HG_CHUNK), dtype=bool))[:, :, None]

    def step(state, xs):
        qc, kc, vc, gc = xs
        b = jnp.cumsum(gc, axis=2)
        o_inter = jnp.einsum('bhck,bhkv->bhcv', qc * jnp.exp(b), state)
        diff = b[:, :, :, None, :] - b[:, :, None, :, :]
        decay = jnp.where(tril, jnp.exp(jnp.minimum(diff, 0.0)), 0.0)
        a = jnp.einsum('bhtk,bhsk,bhtsk->bhts', qc, kc, decay)
        o = o_inter + jnp.einsum('bhts,bhsv->bhtv', a, vc)
        b_last = b[:, :, -1:, :]
        new_state = jnp.exp(b_last[:, :, 0, :])[..., None] * state + jnp.einsum(
            'bhck,bhcv->bhkv', kc * jnp.exp(b_last - b), vc)
        return new_state, o

    init = jnp.zeros((B, H, DK, DV), jnp.float32)
    _, o = lax.scan(step, init, (chunks(q), chunks(k), chunks(v), chunks(logf)))
    return o.transpose(1, 0, 3, 2, 4).reshape(B, S, H, DV)


def hgrn2_bidirectional(q_raw, i_raw, zf_fwd, zf_bwd, lb_f, lb_b, g_norm, gate):
    B, S, _ = q_raw.shape
    q = jax.nn.silu(q_raw.astype(jnp.float32)).reshape(B, S, HG_HEADS, HG_DK)
    v = i_raw.astype(jnp.float32).reshape(B, S, HG_HEADS, HG_DV)

    def gates(z, lb):
        f = lb + (1.0 - lb) * jax.nn.sigmoid(z.astype(jnp.float32))
        return (1.0 - f).reshape(B, S, HG_HEADS, HG_DK), jnp.log(f).reshape(B, S, HG_HEADS, HG_DK)

    k_f, g_f = gates(zf_fwd, lb_f)
    o_f = hgrn2_chunk_scan(q, k_f, v, g_f)
    k_b, g_b = gates(zf_bwd, lb_b)
    flip = lambda t: jnp.flip(t, axis=1)
    o_b = flip(hgrn2_chunk_scan(flip(q), flip(k_b), flip(v), flip(g_b)))
    o = o_f + o_b
    o = o * lax.rsqrt(jnp.mean(o * o, axis=-1, keepdims=True) + NORM_EPS) * g_norm.astype(jnp.float32)
    o = o * jax.nn.silu(gate.astype(jnp.float32)).reshape(B, S, HG_HEADS, HG_DV)
    return o.reshape(B, S, HG_WIDTH)


def trunk(x, w_in, w_out, lb_fwd, lb_bwd, g_hgrn_norm, g_pre_mix, g_post_mix,
          g_pre_ffn, g_post_ffn, w_gate, w_up, w_down):
    B, S, _ = x.shape
    pos = jnp.arange(S, dtype=jnp.float32)
    lb_all_f = jnp.cumsum(jax.nn.softmax(lb_fwd.astype(jnp.float32), axis=0), axis=0)
    lb_all_b = jnp.cumsum(jax.nn.softmax(lb_bwd.astype(jnp.float32), axis=0), axis=0)
    for layer in range(DEPTH):
        h = rmsnorm(x, g_pre_mix[layer])
        u = h @ w_in[layer]
        offs = np.cumsum([0, ATT_WIDTH, ATT_WIDTH, ATT_WIDTH, HG_WIDTH, HG_WIDTH, HG_WIDTH, HG_WIDTH, HG_WIDTH])
        q_a, k_a, v_a, q_h, zf_f, zf_b, i_h, g_h = [u[..., int(offs[j]):int(offs[j + 1])] for j in range(8)]
        qa = partial_rope(q_a.reshape(B, S, ATT_HEADS, ATT_HEAD_DIM), pos)
        ka = partial_rope(k_a.reshape(B, S, ATT_HEADS, ATT_HEAD_DIM), pos)
        va = v_a.reshape(B, S, ATT_HEADS, ATT_HEAD_DIM).astype(jnp.float32)
        att = dilated_attention(qa, ka, va).reshape(B, S, ATT_WIDTH)
        hg = hgrn2_bidirectional(q_h, i_h, zf_f, zf_b, lb_all_f[layer], lb_all_b[layer],
                                 g_hgrn_norm[layer], g_h)
        mix = jnp.concatenate([att, hg], axis=-1).astype(x.dtype) @ w_out[layer]
        x = x + rmsnorm(mix, g_post_mix[layer])
        h = rmsnorm(x, g_pre_ffn[layer])
        ff = (jax.nn.silu(h @ w_gate[layer]) * (h @ w_up[layer])) @ w_down[layer]
        x = x + rmsnorm(ff, g_post_ffn[layer])
    return x


def setup_inputs(seed: int = 0) -> dict:
    key = jax.random.key(seed)
    ks = jax.random.split(key, 16)
    f32 = jnp.float32

    def nrm(k, shape, scale):
        return jax.random.normal(k, shape, f32) * scale

    def gain(k, shape):
        return 1.0 + 0.05 * jax.random.normal(k, shape, f32)

    return {
        "x_prompt": nrm(ks[0], (BATCH, SEQ, D_MODEL), 1.0),
        "x_sample": nrm(ks[1], (DEC_BATCH, DEC_SEQ, D_MODEL), 1.0),
        "w_in": nrm(ks[2], (DEPTH, D_MODEL, IN_WIDTH), D_MODEL ** -0.5),
        "w_out": nrm(ks[3], (DEPTH, MIX_WIDTH, D_MODEL), MIX_WIDTH ** -0.5),
        "lb_fwd": nrm(ks[4], (DEPTH + 1, HG_WIDTH), 0.5),
        "lb_bwd": nrm(ks[5], (DEPTH + 1, HG_WIDTH), 0.5),
        "g_hgrn_norm": gain(ks[6], (DEPTH, HG_DV)),
        "g_pre_mix": gain(ks[7], (DEPTH, D_MODEL)),
        "g_post_mix": gain(ks[8], (DEPTH, D_MODEL)),
        "g_pre_ffn": gain(ks[9], (DEPTH, D_MODEL)),
        "g_post_ffn": gain(ks[10], (DEPTH, D_MODEL)),
        "w_gate": nrm(ks[11], (DEPTH, D_MODEL, D_FF), D_MODEL ** -0.5),
        "w_up": nrm(ks[12], (DEPTH, D_MODEL, D_FF), D_MODEL ** -0.5),
        "w_down": nrm(ks[13], (DEPTH, D_FF, D_MODEL), D_FF ** -0.5),
    }


def reference(x_prompt, x_sample, w_in, w_out, lb_fwd, lb_bwd, g_hgrn_norm, g_pre_mix,
              g_post_mix, g_pre_ffn, g_post_ffn, w_gate, w_up, w_down):
    y_prompt = trunk(x_prompt, w_in, w_out, lb_fwd, lb_bwd, g_hgrn_norm, g_pre_mix, g_post_mix,
                     g_pre_ffn, g_post_ffn, w_gate, w_up, w_down)
    y_sample = trunk(x_sample, w_in, w_out, lb_fwd, lb_bwd, g_hgrn_norm, g_pre_mix, g_post_mix,
                     g_pre_ffn, g_post_ffn, w_gate, w_up, w_down)
    return (y_prompt, y_sample)
```

```python
import functools
import math

import jax
import jax.numpy as jnp
import numpy as np
from jax import lax
from jax.experimental import pallas as pl
from jax.experimental.pallas import tpu as pltpu

F32 = jnp.float32
BF16 = jnp.bfloat16

D_MODEL = 1024
ATT_HEADS = 8
ATT_HEAD_DIM = 64
ATT_WIDTH = ATT_HEADS * ATT_HEAD_DIM
DILATED_BRANCHES = ((128, 1), (512, 4), (2048, 16))
ROT_DIM = ATT_HEAD_DIM // 4
ROPE_THETA = 500000.0
HG_HEADS = 4
HG_DK = 128
HG_WIDTH = HG_HEADS * HG_DK
HG_CHUNK = 64
GROUP_W = 512
N_GROUPS = 8
NORM_EPS = 1e-6
NEG_FILL = -1e30

LANES = 128
SUBLANES = 8
ATT_HALF = 64
ATT_TQ = 128
ATT_TK = ATT_TQ + 2 * ATT_HALF
ATT_TQB_MAX = 1024
SAFE_DECAY_LOG = 80.0
VMEM_LIMIT = 56 * 1024 * 1024


def _sigmoid_pair(z):
    e = jnp.exp(-jnp.abs(z))
    big = 1.0 / (1.0 + e)
    small = e * big
    pos = z >= 0
    return jnp.where(pos, big, small), jnp.where(pos, small, big)


def _silu(z):
    s, _ = _sigmoid_pair(z)
    return z * s


def _rms(x):
    return x * lax.rsqrt(jnp.mean(x * x, axis=-1, keepdims=True) + NORM_EPS)


def _in_proj_kernel(x_ref, g_ref, w_ref, cos_ref, sina_ref, sinb_ref, lbf_ref, lbb_ref,
                    qa_ref, ka_ref, va_ref, qh_ref, gf_ref, kf_ref, gb_ref, kb_ref, vh_ref, gs_ref):
    h = (_rms(x_ref[...]) * g_ref[...]).astype(BF16)

    def proj(j):
        return jnp.dot(h, w_ref[:, j * GROUP_W:(j + 1) * GROUP_W], preferred_element_type=F32)

    reps = GROUP_W // LANES
    cos = jnp.concatenate([cos_ref[...]] * reps, axis=1)
    sina = jnp.concatenate([sina_ref[...]] * reps, axis=1)
    sinb = jnp.concatenate([sinb_ref[...]] * reps, axis=1)
    half = ROT_DIM // 2

    def rope(t):
        return t * cos + pltpu.roll(t, GROUP_W - half, 1) * sina + pltpu.roll(t, half, 1) * sinb

    qa_ref[...] = (rope(proj(0)) * (ATT_HEAD_DIM ** -0.5)).astype(BF16)
    ka_ref[...] = rope(proj(1)).astype(BF16)
    va_ref[...] = proj(2).astype(BF16)
    qh_ref[...] = _silu(proj(3)).astype(BF16)

    def lower_bound(lb_ref):
        a = lb_ref[...]
        e = jnp.exp(a - jnp.max(a, axis=0, keepdims=True))
        return e[0:1, :] / jnp.sum(e, axis=0, keepdims=True)

    def gates(z, lb, g_out, k_out):
        s_pos, s_neg = _sigmoid_pair(z)
        g_out[...] = jnp.log(lb + (1.0 - lb) * s_pos)
        k_out[...] = ((1.0 - lb) * s_neg).astype(BF16)

    gates(proj(4), lower_bound(lbf_ref), gf_ref, kf_ref)
    gates(proj(5), lower_bound(lbb_ref), gb_ref, kb_ref)
    vh_ref[...] = proj(6).astype(BF16)
    gs_ref[...] = _silu(proj(7)).astype(BF16)


def _in_proj(x2, g_pre, w_in, tables, lb_f, lb_b, seq_len, tm):
    n = x2.shape[0]
    n_pos_tiles = seq_len // tm
    row = lambda i: (i, 0)
    const = lambda i: (0, 0)
    pos = lambda i: (i % n_pos_tiles, 0)
    out_dtypes = (BF16, BF16, BF16, BF16, F32, BF16, F32, BF16, BF16, BF16)
    return pl.pallas_call(
        _in_proj_kernel,
        grid=(n // tm,),
        in_specs=[
            pl.BlockSpec((tm, D_MODEL), row),
            pl.BlockSpec((1, D_MODEL), const),
            pl.BlockSpec((D_MODEL, N_GROUPS * GROUP_W), const, pipeline_mode=pl.Buffered(1)),
            pl.BlockSpec((tm, LANES), pos),
            pl.BlockSpec((tm, LANES), pos),
            pl.BlockSpec((tm, LANES), pos),
            pl.BlockSpec(lb_f.shape, const),
            pl.BlockSpec(lb_b.shape, const),
        ],
        out_specs=[pl.BlockSpec((tm, GROUP_W), row)] * len(out_dtypes),
        out_shape=[jax.ShapeDtypeStruct((n, GROUP_W), dt) for dt in out_dtypes],
        compiler_params=pltpu.CompilerParams(
            dimension_semantics=("parallel",), vmem_limit_bytes=VMEM_LIMIT),
    )(x2, g_pre, w_in, *tables, lb_f, lb_b)


def _rope_tables(seq_len):
    pos = jnp.arange(seq_len, dtype=F32)
    inv_freq = ROPE_THETA ** (-jnp.arange(0, ROT_DIM, 2, dtype=F32) / ROT_DIM)
    ang = pos[:, None] * inv_freq[None, :]
    cos, sin = jnp.cos(ang), jnp.sin(ang)
    half = ROT_DIM // 2
    pad = ATT_HEAD_DIM - ROT_DIM
    ones = jnp.ones((seq_len, pad), F32)
    zeros = jnp.zeros((seq_len, pad), F32)
    zh = jnp.zeros((seq_len, half), F32)
    cos_h = jnp.concatenate([cos, cos, ones], axis=1)
    sina_h = jnp.concatenate([-sin, zh, zeros], axis=1)
    sinb_h = jnp.concatenate([zh, sin, zeros], axis=1)
    rep = LANES // ATT_HEAD_DIM
    return tuple(jnp.concatenate([t] * rep, axis=1) for t in (cos_h, sina_h, sinb_h))


def _attn_kernel(q_ref, kp_ref, kc_ref, kn_ref, vp_ref, vc_ref, vn_ref, o_ref, st_ref,
                 kbuf, vbuf, *, sub_len, tqb):
    n = pl.program_id(2)
    kbuf[0:ATT_HALF, :] = kp_ref[...]
    kbuf[ATT_HALF:ATT_HALF + tqb, :] = kc_ref[...]
    kbuf[ATT_HALF + tqb:, :] = kn_ref[...]
    vbuf[0:ATT_HALF, :] = vp_ref[...]
    vbuf[ATT_HALF:ATT_HALF + tqb, :] = vc_ref[...]
    vbuf[ATT_HALF + tqb:, :] = vn_ref[...]

    heads_per_group = LANES // ATT_HEAD_DIM
    rows = heads_per_group * ATT_TQ
    qi = lax.broadcasted_iota(jnp.int32, (rows, ATT_TK), 0) % ATT_TQ
    kj = lax.broadcasted_iota(jnp.int32, (rows, ATT_TK), 1)
    in_band = (kj >= qi) & (kj <= qi + 2 * ATT_HALF)
    lane = lax.broadcasted_iota(jnp.int32, (ATT_TQ, LANES), 1)
    first_head = lane < ATT_HEAD_DIM
    ones = jnp.ones((ATT_TK, LANES), BF16)

    def sub_tile(i, carry):
        r0 = pl.multiple_of(i * ATT_TQ, ATT_TQ)
        key0 = n * tqb + i * ATT_TQ - ATT_HALF
        valid = in_band & (kj >= -key0) & (kj < sub_len - key0)
        bias = jnp.where(valid, 0.0, NEG_FILL)
        stats = jnp.zeros((ATT_TQ, LANES), F32)
        for g in range(ATT_WIDTH // LANES):
            cols = slice(g * LANES, (g + 1) * LANES)
            q2 = q_ref[pl.ds(r0, ATT_TQ), cols]
            zero = jnp.zeros_like(q2)
            qs = jnp.concatenate([jnp.where(first_head, q2, zero), jnp.where(first_head, zero, q2)], axis=0)
            kk = kbuf[pl.ds(r0, ATT_TK), cols]
            s = lax.dot_general(qs, kk, (((1,), (1,)), ((), ())), preferred_element_type=F32) + bias
            m = jnp.max(s, axis=-1, keepdims=True)
            p = jnp.exp(s - m).astype(BF16)
            vext = jnp.concatenate([vbuf[pl.ds(r0, ATT_TK), cols], ones], axis=1)
            r = jnp.dot(p, vext, preferred_element_type=F32)
            o0, l0 = r[:ATT_TQ, :LANES], r[:ATT_TQ, LANES:]
            o1, l1 = r[ATT_TQ:, :LANES], r[ATT_TQ:, LANES:]
            o_ref[pl.ds(r0, ATT_TQ), cols] = jnp.where(first_head, o0 / l0, o1 / l1).astype(o_ref.dtype)
            lse0 = m[:ATT_TQ] + jnp.log(l0)
            lse1 = m[ATT_TQ:] + jnp.log(l1)
            stats = jnp.where(lane == heads_per_group * g, lse0, stats)
            stats = jnp.where(lane == heads_per_group * g + 1, lse1, stats)
        st_ref[pl.ds(r0, ATT_TQ), :] = stats
        return carry

    lax.fori_loop(0, tqb // ATT_TQ, sub_tile, 0)


def _attn_branch(q, k, v, batch, seq_len, dil):
    sub_len = seq_len // dil
    tqb = min(sub_len, ATT_TQB_MAX)
    assert sub_len % tqb == 0 and tqb % ATT_TQ == 0
    halo_per_tile = tqb // ATT_HALF
    n_halo = sub_len // ATT_HALF
    view = lambda t: t.reshape(batch, sub_len, dil * ATT_WIDTH)
    cur = lambda b, r, n: (b, n, r)
    prev = lambda b, r, n: (b, jnp.maximum(n * halo_per_tile - 1, 0), r)
    nxt = lambda b, r, n: (b, jnp.minimum((n + 1) * halo_per_tile, n_halo - 1), r)
    main = pl.BlockSpec((None, tqb, ATT_WIDTH), cur)
    halo_p = pl.BlockSpec((None, ATT_HALF, ATT_WIDTH), prev)
    halo_n = pl.BlockSpec((None, ATT_HALF, ATT_WIDTH), nxt)
    o, st = pl.pallas_call(
        functools.partial(_attn_kernel, sub_len=sub_len, tqb=tqb),
        grid=(batch, dil, sub_len // tqb),
        in_specs=[main, halo_p, main, halo_n, halo_p, main, halo_n],
        out_specs=[main, pl.BlockSpec((None, tqb, LANES), cur)],
        out_shape=[jax.ShapeDtypeStruct((batch, sub_len, dil * ATT_WIDTH), BF16),
                   jax.ShapeDtypeStruct((batch, sub_len, dil * LANES), F32)],
        scratch_shapes=[pltpu.VMEM((tqb + 2 * ATT_HALF, ATT_WIDTH), BF16)] * 2,
        compiler_params=pltpu.CompilerParams(
            dimension_semantics=("parallel", "parallel", "parallel"), vmem_limit_bytes=VMEM_LIMIT),
    )(view(q), view(k), view(k), view(k), view(v), view(v), view(v))
    return o.reshape(batch * seq_len, ATT_WIDTH), st.reshape(batch * seq_len, LANES)


def _hgrn_kernel(tri_f_ref, tri_b_ref,
                 qf_ref, gf_ref, kf_ref, vf_ref, qb_ref, gb_ref, kb_ref, vb_ref,
                 of_ref, ob_ref,
                 st_f, st_b, b_f, b_b, a_buf, k32, *, th):
    n_chunks = th // HG_CHUNK

    @pl.when(pl.program_id(1) == 0)
    def _():
        st_f[...] = jnp.zeros_like(st_f)
        st_b[...] = jnp.zeros_like(st_b)

    def chunk_sums(tri_ref, g_ref, out):
        g = g_ref[...]
        hi = g.astype(BF16)
        r1 = g - hi.astype(F32)
        mid = r1.astype(BF16)
        lo = (r1 - mid.astype(F32)).astype(BF16)
        tri = tri_ref[...]
        out[...] = (jnp.dot(tri, hi, preferred_element_type=F32)
                    + jnp.dot(tri, mid, preferred_element_type=F32)
                    + jnp.dot(tri, lo, preferred_element_type=F32))

    chunk_sums(tri_f_ref, gf_ref, b_f)
    chunk_sums(tri_b_ref, gb_ref, b_b)

    ti = lax.broadcasted_iota(jnp.int32, (HG_CHUNK, HG_CHUNK), 0)
    si = lax.broadcasted_iota(jnp.int32, (HG_CHUNK, HG_CHUNK), 1)
    dirs = (
        (False, qf_ref, kf_ref, vf_ref, b_f, of_ref, st_f, si <= ti, HG_CHUNK // 2 - 1, HG_CHUNK - 1),
        (True, qb_ref, kb_ref, vb_ref, b_b, ob_ref, st_b, si >= ti, HG_CHUNK // 2, 0),
    )

    safe = jnp.minimum(jnp.min(b_f[...]), jnp.min(b_b[...])) >= -SAFE_DECAY_LOG

    def slot(d, c, h):
        return (d * n_chunks + c) * HG_HEADS + h

    @pl.when(safe)
    def _():
        for d, (_, q_ref, k_ref, _, b, _, _, mask, anchor, _) in enumerate(dirs):
            for c in range(n_chunks):
                rows = slice(c * HG_CHUNK, (c + 1) * HG_CHUNK)
                for h in range(HG_HEADS):
                    cols = slice(h * HG_DK, (h + 1) * HG_DK)
                    bc = b[rows, cols]
                    mid = bc[anchor:anchor + 1, :]
                    qa = (q_ref[rows, cols].astype(F32) * jnp.exp(bc - mid)).astype(BF16)
                    ka = (k_ref[rows, cols].astype(F32) * jnp.exp(mid - bc)).astype(BF16)
                    a = lax.dot_general(qa, ka, (((1,), (1,)), ((), ())), preferred_element_type=F32)
                    a_buf[slot(d, c, h)] = jnp.where(mask, a, 0.0)

    @pl.when(jnp.logical_not(safe))
    def _():
        for d, (_, q_ref, k_ref, _, b, _, _, mask, _, _) in enumerate(dirs):
            k32[...] = k_ref[...].astype(F32)
            for h in range(HG_HEADS):
                cols = slice(h * HG_DK, (h + 1) * HG_DK)

                def per_chunk(c, carry):
                    r0 = pl.multiple_of(c * HG_CHUNK, HG_CHUNK)
                    bc = b[pl.ds(r0, HG_CHUNK), cols]
                    qc = q_ref[pl.ds(r0, HG_CHUNK), cols].astype(F32)

                    def per_key_group(s8, a):
                        k0 = pl.multiple_of(r0 + s8 * SUBLANES, SUBLANES)
                        b8 = b[pl.ds(k0, SUBLANES), cols]
                        k8 = k32[pl.ds(k0, SUBLANES), cols]
                        for j in range(SUBLANES):
                            w = qc * k8[j:j + 1, :] * jnp.exp(jnp.minimum(bc - b8[j:j + 1, :], 0.0))
                            a = jnp.where(si == s8 * SUBLANES + j, jnp.sum(w, axis=-1, keepdims=True), a)
                        return a

                    a = lax.fori_loop(0, HG_CHUNK // SUBLANES, per_key_group,
                                      jnp.zeros((HG_CHUNK, HG_CHUNK), F32))
                    a_buf[slot(d, c, h)] = jnp.where(mask, a, 0.0)
                    return carry

                lax.fori_loop(0, n_chunks, per_chunk, 0)

    for d, (rev, q_ref, k_ref, v_ref, b, o_ref, st, _, _, edge) in enumerate(dirs):
        order = range(n_chunks - 1, -1, -1) if rev else range(n_chunks)
        for c in order:
            rows = slice(c * HG_CHUNK, (c + 1) * HG_CHUNK)
            for h in range(HG_HEADS):
                cols = slice(h * HG_DK, (h + 1) * HG_DK)
                bc = b[rows, cols]
                b_edge = bc[edge:edge + 1, :]
                v = v_ref[rows, cols]
                qi = (q_ref[rows, cols].astype(F32) * jnp.exp(bc)).astype(BF16)
                kd = (k_ref[rows, cols].astype(F32) * jnp.exp(b_edge - bc)).astype(BF16)
                state = st[h]
                o = lax.dot_general(qi, state.astype(BF16), (((1,), (1,)), ((), ())),
                                    preferred_element_type=F32)
                o = o + jnp.dot(a_buf[slot(d, c, h)].astype(BF16), v, preferred_element_type=F32)
                o_ref[rows, cols] = o.astype(o_ref.dtype)
                st[h] = state * jnp.exp(b_edge) + lax.dot_general(
                    v, kd, (((0,), (0,)), ((), ())), preferred_element_type=F32)


def _block_triangular(th, upper):
    t = np.arange(th)
    same = (t[:, None] // HG_CHUNK) == (t[None, :] // HG_CHUNK)
    tri = (t[None, :] >= t[:, None]) if upper else (t[None, :] <= t[:, None])
    return jnp.asarray(same & tri, dtype=BF16)


def _hgrn(qh, gf, kf, gb, kb, vh, batch, seq_len, th):
    nt = seq_len // th
    view = lambda t: t.reshape(batch, seq_len, HG_WIDTH)
    fwd = lambda b, i: (b, i, 0)
    bwd = lambda b, i: (b, nt - 1 - i, 0)
    const = lambda b, i: (0, 0)
    tile_f = pl.BlockSpec((None, th, HG_WIDTH), fwd)
    tile_b = pl.BlockSpec((None, th, HG_WIDTH), bwd)
    tri = pl.BlockSpec((th, th), const)
    n_slots = 2 * (th // HG_CHUNK) * HG_HEADS
    of, ob = pl.pallas_call(
        functools.partial(_hgrn_kernel, th=th),
        grid=(batch, nt),
        in_specs=[tri, tri, tile_f, tile_f, tile_f, tile_f, tile_b, tile_b, tile_b, tile_b],
        out_specs=[tile_f, tile_b],
        out_shape=[jax.ShapeDtypeStruct((batch, seq_len, HG_WIDTH), BF16)] * 2,
        scratch_shapes=[
            pltpu.VMEM((HG_HEADS, HG_DK, HG_DK), F32),
            pltpu.VMEM((HG_HEADS, HG_DK, HG_DK), F32),
            pltpu.VMEM((th, HG_WIDTH), F32),
            pltpu.VMEM((th, HG_WIDTH), F32),
            pltpu.VMEM((n_slots, HG_CHUNK, HG_CHUNK), F32),
            pltpu.VMEM((th, HG_WIDTH), F32),
        ],
        compiler_params=pltpu.CompilerParams(
            dimension_semantics=("parallel", "arbitrary"), vmem_limit_bytes=VMEM_LIMIT),
    )(_block_triangular(th, False), _block_triangular(th, True),
      view(qh), view(gf), view(kf), view(vh), view(qh), view(gb), view(kb), view(vh))
    n = batch * seq_len
    return of.reshape(n, HG_WIDTH), ob.reshape(n, HG_WIDTH)


def _out_ffn_kernel(x_ref, o1_ref, o2_ref, o3_ref, s1_ref, s2_ref, s3_ref, hf_ref, hb_ref, gs_ref,
                    expand_ref, gh_ref, wout_ref, gpm_ref, gpf_ref, gpo_ref, wg_ref, wu_ref, wd_ref,
                    y_ref):
    lses = (s1_ref[...], s2_ref[...], s3_ref[...])
    top = jnp.maximum(jnp.maximum(lses[0], lses[1]), lses[2])
    es = [jnp.exp(s - top) for s in lses]
    den = es[0] + es[1] + es[2]
    att = None
    for e, o_ref in zip(es, (o1_ref, o2_ref, o3_ref)):
        w = jnp.dot((e / den).astype(BF16), expand_ref[...], preferred_element_type=F32)
        term = w * o_ref[...].astype(F32)
        att = term if att is None else att + term

    o = hf_ref[...].astype(F32) + hb_ref[...].astype(F32)
    hg = jnp.concatenate(
        [_rms(o[:, h * HG_DK:(h + 1) * HG_DK]) * gh_ref[...] for h in range(HG_HEADS)], axis=1)
    hg = hg * gs_ref[...].astype(F32)

    mix_in = jnp.concatenate([att, hg], axis=1).astype(BF16)
    mix = jnp.dot(mix_in, wout_ref[...], preferred_element_type=F32)
    x1 = x_ref[...] + _rms(mix) * gpm_ref[...]

    h2 = (_rms(x1) * gpf_ref[...]).astype(BF16)
    gate = jnp.dot(h2, wg_ref[...], preferred_element_type=F32)
    up = jnp.dot(h2, wu_ref[...], preferred_element_type=F32)
    act = (_silu(gate) * up).astype(BF16)
    ff = jnp.dot(act, wd_ref[...], preferred_element_type=F32)
    y_ref[...] = x1 + _rms(ff) * gpo_ref[...]


def _out_ffn(x2, att_o, att_s, hf, hb, gs, g_hnorm, w_out, g_pm, g_pf, g_po, w_gate, w_up, w_down, tm):
    n = x2.shape[0]
    d_ff = w_gate.shape[1]
    lane_head = np.arange(LANES)[:, None]
    col_head = np.arange(ATT_WIDTH)[None, :] // ATT_HEAD_DIM
    expand = jnp.asarray(lane_head == col_head, dtype=BF16)
    row = lambda i: (i, 0)
    const = lambda i: (0, 0)
    tile = lambda w: pl.BlockSpec((tm, w), row)
    whole = lambda a: pl.BlockSpec(a.shape, const, pipeline_mode=pl.Buffered(1))
    consts = (expand, g_hnorm, w_out, g_pm, g_pf, g_po, w_gate, w_up, w_down)
    return pl.pallas_call(
        _out_ffn_kernel,
        grid=(n // tm,),
        in_specs=[tile(D_MODEL)] + [tile(ATT_WIDTH)] * 3 + [tile(LANES)] * 3 + [tile(HG_WIDTH)] * 3
                 + [whole(a) for a in consts],
        out_specs=tile(D_MODEL),
        out_shape=jax.ShapeDtypeStruct((n, D_MODEL), F32),
        compiler_params=pltpu.CompilerParams(
            dimension_semantics=("parallel",), vmem_limit_bytes=VMEM_LIMIT),
    )(x2, *att_o, *att_s, hf, hb, gs, *consts)


def _layer(x, w_in, w_out, lb_fwd, lb_bwd, g_hnorm, g_pre_mix, g_post_mix, g_pre_ffn, g_post_ffn,
           w_gate, w_up, w_down, *, tm_in=512, th=256, tm_out=256):
    batch, seq_len, _ = x.shape
    x2 = x.reshape(batch * seq_len, D_MODEL)
    qa, ka, va, qh, gf, kf, gb, kb, vh, gs = _in_proj(
        x2, g_pre_mix, w_in, _rope_tables(seq_len), lb_fwd, lb_bwd, seq_len, tm_in)
    att = [_attn_branch(qa, ka, va, batch, seq_len, dil) for _, dil in DILATED_BRANCHES]
    hf, hb = _hgrn(qh, gf, kf, gb, kb, vh, batch, seq_len, th)
    y = _out_ffn(x2, [o for o, _ in att], [s for _, s in att], hf, hb, gs, g_hnorm, w_out,
                 g_post_mix, g_pre_ffn, g_post_ffn, w_gate, w_up, w_down, tm_out)
    return y.reshape(batch, seq_len, D_MODEL)


def kernel(x_prompt, x_sample, w_in, w_out, lb_fwd, lb_bwd, g_hgrn_norm, g_pre_mix, g_post_mix,
           g_pre_ffn, g_post_ffn, w_gate, w_up, w_down):
    assert w_in.shape[0] == 1, "one layer"
    assert all(w // (2 * d) == ATT_HALF for w, d in DILATED_BRANCHES)
    params = (w_in[0].astype(BF16), w_out[0].astype(BF16), lb_fwd, lb_bwd, g_hgrn_norm,
              g_pre_mix, g_post_mix, g_pre_ffn, g_post_ffn,
              w_gate[0].astype(BF16), w_up[0].astype(BF16), w_down[0].astype(BF16))
    return _layer(x_prompt, *params), _layer(x_sample, *params)
```

```python
import functools
import math

import jax
import jax.numpy as jnp
import numpy as np
from jax import lax
from jax.experimental import pallas as pl
from jax.experimental.pallas import tpu as pltpu

F32 = jnp.float32
BF16 = jnp.bfloat16

D_MODEL = 1024
ATT_HEADS = 8
ATT_HEAD_DIM = 64
ATT_WIDTH = ATT_HEADS * ATT_HEAD_DIM
DILATED_BRANCHES = ((128, 1), (512, 4), (2048, 16))
ROT_DIM = ATT_HEAD_DIM // 4
ROPE_THETA = 500000.0
HG_HEADS = 4
HG_DK = 128
HG_WIDTH = HG_HEADS * HG_DK
HG_CHUNK = 64
GROUP_W = 512
N_GROUPS = 8
NORM_EPS = 1e-6
NEG_FILL = -1e30

LANES = 128
SUBLANES = 8
ATT_HALF = 64
ATT_TQ = 128
ATT_TK = ATT_TQ + 2 * ATT_HALF
ATT_QUERIES_PER_STEP = 1024
ATT_POS_PER_STEP = (1024, 4096)
SAFE_DECAY_LOG = 80.0
VMEM_LIMIT = 56 * 1024 * 1024


def _sigmoid_pair(z):
    e = jnp.exp(-jnp.abs(z))
    big = 1.0 / (1.0 + e)
    small = e * big
    pos = z >= 0
    return jnp.where(pos, big, small), jnp.where(pos, small, big)


def _silu(z):
    s, _ = _sigmoid_pair(z)
    return z * s


def _rms(x):
    return x * lax.rsqrt(jnp.mean(x * x, axis=-1, keepdims=True) + NORM_EPS)


def _in_proj_kernel(x_ref, g_ref, w_ref, cos_ref, sina_ref, sinb_ref, lbf_ref, lbb_ref, *refs, tm):
    n_att = 3 * len(DILATED_BRANCHES)
    att_refs = refs[:n_att]
    qh_ref, gf_ref, kf_ref, gb_ref, kb_ref, vh_ref, gs_ref, stage = refs[n_att:]
    h = (_rms(x_ref[...]) * g_ref[...]).astype(BF16)
    slabs = GROUP_W // LANES

    def emit_dilated(t, which):
        for b, (_, dil) in enumerate(DILATED_BRANCHES):
            out = att_refs[3 * b + which]
            if dil == 1:
                out[...] = t.astype(BF16)
        for s in range(slabs):
            stage[s] = t[:, s * LANES:(s + 1) * LANES]
        for b, (_, dil) in enumerate(DILATED_BRANCHES):
            out = att_refs[3 * b + which]
            if dil == 1:
                continue
            for r in range(dil):
                for s in range(slabs):
                    lane0 = r * GROUP_W + s * LANES
                    out[:, lane0:lane0 + LANES] = stage[s, pl.ds(r, tm // dil, stride=dil), :].astype(BF16)

    def proj(j):
        return jnp.dot(h, w_ref[:, j * GROUP_W:(j + 1) * GROUP_W], preferred_element_type=F32)

    reps = GROUP_W // LANES
    cos = jnp.concatenate([cos_ref[...]] * reps, axis=1)
    sina = jnp.concatenate([sina_ref[...]] * reps, axis=1)
    sinb = jnp.concatenate([sinb_ref[...]] * reps, axis=1)
    half = ROT_DIM // 2

    def rope(t):
        return t * cos + pltpu.roll(t, GROUP_W - half, 1) * sina + pltpu.roll(t, half, 1) * sinb

    emit_dilated(rope(proj(0)) * (ATT_HEAD_DIM ** -0.5), 0)
    emit_dilated(rope(proj(1)), 1)
    emit_dilated(proj(2), 2)
    qh_ref[...] = _silu(proj(3)).astype(BF16)

    def lower_bound(lb_ref):
        a = lb_ref[...]
        e = jnp.exp(a - jnp.max(a, axis=0, keepdims=True))
        return e[0:1, :] / jnp.sum(e, axis=0, keepdims=True)

    def gates(z, lb, g_out, k_out):
        s_pos, s_neg = _sigmoid_pair(z)
        g_out[...] = jnp.log(lb + (1.0 - lb) * s_pos)
        k_out[...] = ((1.0 - lb) * s_neg).astype(BF16)

    gates(proj(4), lower_bound(lbf_ref), gf_ref, kf_ref)
    gates(proj(5), lower_bound(lbb_ref), gb_ref, kb_ref)
    vh_ref[...] = proj(6).astype(BF16)
    gs_ref[...] = _silu(proj(7)).astype(BF16)


def _in_proj(x2, g_pre, w_in, tables, lb_f, lb_b, seq_len, tm):
    n = x2.shape[0]
    n_pos_tiles = seq_len // tm
    row = lambda i: (i, 0)
    const = lambda i: (0, 0)
    pos = lambda i: (i % n_pos_tiles, 0)
    hg_dtypes = (BF16, F32, BF16, F32, BF16, BF16, BF16)
    att_specs, att_shapes = [], []
    for _, dil in DILATED_BRANCHES:
        att_specs += [pl.BlockSpec((tm // dil, dil * GROUP_W), row)] * 3
        att_shapes += [jax.ShapeDtypeStruct((n // dil, dil * GROUP_W), BF16)] * 3
    return pl.pallas_call(
        functools.partial(_in_proj_kernel, tm=tm),
        grid=(n // tm,),
        in_specs=[
            pl.BlockSpec((tm, D_MODEL), row),
            pl.BlockSpec((1, D_MODEL), const),
            pl.BlockSpec((D_MODEL, N_GROUPS * GROUP_W), const, pipeline_mode=pl.Buffered(1)),
            pl.BlockSpec((tm, LANES), pos),
            pl.BlockSpec((tm, LANES), pos),
            pl.BlockSpec((tm, LANES), pos),
            pl.BlockSpec(lb_f.shape, const),
            pl.BlockSpec(lb_b.shape, const),
        ],
        out_specs=att_specs + [pl.BlockSpec((tm, GROUP_W), row)] * len(hg_dtypes),
        out_shape=att_shapes + [jax.ShapeDtypeStruct((n, GROUP_W), dt) for dt in hg_dtypes],
        scratch_shapes=[pltpu.VMEM((GROUP_W // LANES, tm, LANES), F32)],
        compiler_params=pltpu.CompilerParams(
            dimension_semantics=("parallel",), vmem_limit_bytes=VMEM_LIMIT),
    )(x2, g_pre, w_in, *tables, lb_f, lb_b)


def _rope_tables(seq_len):
    pos = jnp.arange(seq_len, dtype=F32)
    inv_freq = ROPE_THETA ** (-jnp.arange(0, ROT_DIM, 2, dtype=F32) / ROT_DIM)
    ang = pos[:, None] * inv_freq[None, :]
    cos, sin = jnp.cos(ang), jnp.sin(ang)
    half = ROT_DIM // 2
    pad = ATT_HEAD_DIM - ROT_DIM
    ones = jnp.ones((seq_len, pad), F32)
    zeros = jnp.zeros((seq_len, pad), F32)
    zh = jnp.zeros((seq_len, half), F32)
    cos_h = jnp.concatenate([cos, cos, ones], axis=1)
    sina_h = jnp.concatenate([-sin, zh, zeros], axis=1)
    sinb_h = jnp.concatenate([zh, sin, zeros], axis=1)
    rep = LANES // ATT_HEAD_DIM
    return tuple(jnp.concatenate([t] * rep, axis=1) for t in (cos_h, sina_h, sinb_h))


def _attn_kernel(q_ref, kp_ref, kc_ref, kn_ref, vp_ref, vc_ref, vn_ref, o_ref, st_ref,
                 kbuf, vbuf, *, sub_len, tqs, dil, rps):
    n = pl.program_id(1)
    heads_per_group = LANES // ATT_HEAD_DIM
    rows = heads_per_group * ATT_TQ
    qi = lax.broadcasted_iota(jnp.int32, (rows, ATT_TK), 0) % ATT_TQ
    kj = lax.broadcasted_iota(jnp.int32, (rows, ATT_TK), 1)
    in_band = (kj >= qi) & (kj <= qi + 2 * ATT_HALF)
    lane = lax.broadcasted_iota(jnp.int32, (ATT_TQ, LANES), 1)
    first_head = lane < ATT_HEAD_DIM
    ones = jnp.ones((ATT_TK, LANES), BF16)

    for j in range(rps):
        res = pl.program_id(2) * rps + j
        lanes_j = slice(j * ATT_WIDTH, (j + 1) * ATT_WIDTH)
        kbuf[0:ATT_HALF, :] = kp_ref[:, lanes_j]
        kbuf[ATT_HALF:ATT_HALF + tqs, :] = kc_ref[:, lanes_j]
        kbuf[ATT_HALF + tqs:, :] = kn_ref[:, lanes_j]
        vbuf[0:ATT_HALF, :] = vp_ref[:, lanes_j]
        vbuf[ATT_HALF:ATT_HALF + tqs, :] = vc_ref[:, lanes_j]
        vbuf[ATT_HALF + tqs:, :] = vn_ref[:, lanes_j]

        def sub_tile(i, carry, j=j, res=res):
            r0 = pl.multiple_of(i * ATT_TQ, ATT_TQ)
            key0 = n * tqs + i * ATT_TQ - ATT_HALF
            valid = in_band & (kj >= -key0) & (kj < sub_len - key0)
            bias = jnp.where(valid, 0.0, NEG_FILL)
            if dil == 1:
                rows_out = pl.ds(r0, ATT_TQ)
            else:
                rows_out = pl.ds(r0 * dil + res, ATT_TQ, stride=dil)
            stats = jnp.zeros((ATT_TQ, LANES), F32)
            for g in range(ATT_WIDTH // LANES):
                cols = slice(g * LANES, (g + 1) * LANES)
                q2 = q_ref[pl.ds(r0, ATT_TQ), j * ATT_WIDTH + g * LANES:j * ATT_WIDTH + (g + 1) * LANES]
                zero = jnp.zeros_like(q2)
                qs = jnp.concatenate(
                    [jnp.where(first_head, q2, zero), jnp.where(first_head, zero, q2)], axis=0)
                kk = kbuf[pl.ds(r0, ATT_TK), cols]
                s = lax.dot_general(qs, kk, (((1,), (1,)), ((), ())), preferred_element_type=F32) + bias
                m = jnp.max(s, axis=-1, keepdims=True)
                p = jnp.exp(s - m).astype(BF16)
                vext = jnp.concatenate([vbuf[pl.ds(r0, ATT_TK), cols], ones], axis=1)
                r = jnp.dot(p, vext, preferred_element_type=F32)
                o0, l0 = r[:ATT_TQ, :LANES], r[:ATT_TQ, LANES:]
                o1, l1 = r[ATT_TQ:, :LANES], r[ATT_TQ:, LANES:]
                o_ref[g, rows_out, :] = jnp.where(first_head, o0 / l0, o1 / l1)
                lse0 = m[:ATT_TQ] + jnp.log(l0)
                lse1 = m[ATT_TQ:] + jnp.log(l1)
                stats = jnp.where(lane == heads_per_group * g, lse0, stats)
                stats = jnp.where(lane == heads_per_group * g + 1, lse1, stats)
            st_ref[rows_out, :] = stats
            return carry

        lax.fori_loop(0, tqs // ATT_TQ, sub_tile, 0)


def _attn_branch(q, k, v, batch, seq_len, dil):
    sub_len = seq_len // dil
    t_pos = min(seq_len, ATT_POS_PER_STEP[dil > 1])
    tqs = t_pos // dil
    rps = max(1, min(dil, ATT_QUERIES_PER_STEP // tqs))
    assert seq_len % t_pos == 0 and tqs % ATT_TQ == 0 and dil % rps == 0
    halo_per_tile = tqs // ATT_HALF
    n_halo = sub_len // ATT_HALF
    view = lambda t: t.reshape(batch, sub_len, dil * ATT_WIDTH)
    cur = lambda b, n, r: (b, n, r)
    prev = lambda b, n, r: (b, jnp.maximum(n * halo_per_tile - 1, 0), r)
    nxt = lambda b, n, r: (b, jnp.minimum((n + 1) * halo_per_tile, n_halo - 1), r)
    main = pl.BlockSpec((None, tqs, rps * ATT_WIDTH), cur)
    halo_p = pl.BlockSpec((None, ATT_HALF, rps * ATT_WIDTH), prev)
    halo_n = pl.BlockSpec((None, ATT_HALF, rps * ATT_WIDTH), nxt)
    slabs = ATT_WIDTH // LANES
    return pl.pallas_call(
        functools.partial(_attn_kernel, sub_len=sub_len, tqs=tqs, dil=dil, rps=rps),
        grid=(batch, seq_len // t_pos, dil // rps),
        in_specs=[main, halo_p, main, halo_n, halo_p, main, halo_n],
        out_specs=[pl.BlockSpec((None, slabs, t_pos, LANES), lambda b, n, r: (b, 0, n, 0)),
                   pl.BlockSpec((None, t_pos, LANES), lambda b, n, r: (b, n, 0))],
        out_shape=[jax.ShapeDtypeStruct((batch, slabs, seq_len, LANES), F32),
                   jax.ShapeDtypeStruct((batch, seq_len, LANES), F32)],
        scratch_shapes=[pltpu.VMEM((tqs + 2 * ATT_HALF, ATT_WIDTH), BF16)] * 2,
        compiler_params=pltpu.CompilerParams(
            dimension_semantics=("parallel", "parallel", "arbitrary"), vmem_limit_bytes=VMEM_LIMIT),
    )(view(q), view(k), view(k), view(k), view(v), view(v), view(v))


def _hgrn_kernel(tri_f_ref, tri_b_ref,
                 qf_ref, gf_ref, kf_ref, vf_ref, qb_ref, gb_ref, kb_ref, vb_ref,
                 of_ref, ob_ref,
                 st_f, st_b, b_f, b_b, a_buf, k32, *, th):
    n_chunks = th // HG_CHUNK

    @pl.when(pl.program_id(1) == 0)
    def _():
        st_f[...] = jnp.zeros_like(st_f)
        st_b[...] = jnp.zeros_like(st_b)

    def chunk_sums(tri_ref, g_ref, out):
        g = g_ref[...]
        hi = g.astype(BF16)
        r1 = g - hi.astype(F32)
        mid = r1.astype(BF16)
        lo = (r1 - mid.astype(F32)).astype(BF16)
        tri = tri_ref[...]
        out[...] = (jnp.dot(tri, hi, preferred_element_type=F32)
                    + jnp.dot(tri, mid, preferred_element_type=F32)
                    + jnp.dot(tri, lo, preferred_element_type=F32))

    chunk_sums(tri_f_ref, gf_ref, b_f)
    chunk_sums(tri_b_ref, gb_ref, b_b)

    ti = lax.broadcasted_iota(jnp.int32, (HG_CHUNK, HG_CHUNK), 0)
    si = lax.broadcasted_iota(jnp.int32, (HG_CHUNK, HG_CHUNK), 1)
    dirs = (
        (False, qf_ref, kf_ref, vf_ref, b_f, of_ref, st_f, si <= ti, HG_CHUNK // 2 - 1, HG_CHUNK - 1),
        (True, qb_ref, kb_ref, vb_ref, b_b, ob_ref, st_b, si >= ti, HG_CHUNK // 2, 0),
    )

    safe = jnp.minimum(jnp.min(b_f[...]), jnp.min(b_b[...])) >= -SAFE_DECAY_LOG

    def slot(d, c, h):
        return (d * n_chunks + c) * HG_HEADS + h

    @pl.when(safe)
    def _():
        for d, (_, q_ref, k_ref, _, b, _, _, mask, anchor, _) in enumerate(dirs):
            for c in range(n_chunks):
                rows = slice(c * HG_CHUNK, (c + 1) * HG_CHUNK)
                for h in range(HG_HEADS):
                    cols = slice(h * HG_DK, (h + 1) * HG_DK)
                    bc = b[rows, cols]
                    mid = bc[anchor:anchor + 1, :]
                    qa = (q_ref[rows, cols].astype(F32) * jnp.exp(bc - mid)).astype(BF16)
                    ka = (k_ref[rows, cols].astype(F32) * jnp.exp(mid - bc)).astype(BF16)
                    a = lax.dot_general(qa, ka, (((1,), (1,)), ((), ())), preferred_element_type=F32)
                    a_buf[slot(d, c, h)] = jnp.where(mask, a, 0.0)

    @pl.when(jnp.logical_not(safe))
    def _():
        for d, (_, q_ref, k_ref, _, b, _, _, mask, _, _) in enumerate(dirs):
            k32[...] = k_ref[...].astype(F32)
            for h in range(HG_HEADS):
                cols = slice(h * HG_DK, (h + 1) * HG_DK)

                def per_chunk(c, carry):
                    r0 = pl.multiple_of(c * HG_CHUNK, HG_CHUNK)
                    bc = b[pl.ds(r0, HG_CHUNK), cols]
                    qc = q_ref[pl.ds(r0, HG_CHUNK), cols].astype(F32)

                    def per_key_group(s8, a):
                        k0 = pl.multiple_of(r0 + s8 * SUBLANES, SUBLANES)
                        b8 = b[pl.ds(k0, SUBLANES), cols]
                        k8 = k32[pl.ds(k0, SUBLANES), cols]
                        for j in range(SUBLANES):
                            w = qc * k8[j:j + 1, :] * jnp.exp(jnp.minimum(bc - b8[j:j + 1, :], 0.0))
                            a = jnp.where(si == s8 * SUBLANES + j, jnp.sum(w, axis=-1, keepdims=True), a)
                        return a

                    a = lax.fori_loop(0, HG_CHUNK // SUBLANES, per_key_group,
                                      jnp.zeros((HG_CHUNK, HG_CHUNK), F32))
                    a_buf[slot(d, c, h)] = jnp.where(mask, a, 0.0)
                    return carry

                lax.fori_loop(0, n_chunks, per_chunk, 0)

    for d, (rev, q_ref, k_ref, v_ref, b, o_ref, st, _, _, edge) in enumerate(dirs):
        order = range(n_chunks - 1, -1, -1) if rev else range(n_chunks)
        for c in order:
            rows = slice(c * HG_CHUNK, (c + 1) * HG_CHUNK)
            for h in range(HG_HEADS):
                cols = slice(h * HG_DK, (h + 1) * HG_DK)
                bc = b[rows, cols]
                b_edge = bc[edge:edge + 1, :]
                v = v_ref[rows, cols]
                qi = (q_ref[rows, cols].astype(F32) * jnp.exp(bc)).astype(BF16)
                kd = (k_ref[rows, cols].astype(F32) * jnp.exp(b_edge - bc)).astype(BF16)
                state = st[h]
                o = lax.dot_general(qi, state.astype(BF16), (((1,), (1,)), ((), ())),
                                    preferred_element_type=F32)
                o = o + jnp.dot(a_buf[slot(d, c, h)].astype(BF16), v, preferred_element_type=F32)
                o_ref[rows, cols] = o.astype(o_ref.dtype)
                st[h] = state * jnp.exp(b_edge) + lax.dot_general(
                    v, kd, (((0,), (0,)), ((), ())), preferred_element_type=F32)


def _block_triangular(th, upper):
    t = np.arange(th)
    same = (t[:, None] // HG_CHUNK) == (t[None, :] // HG_CHUNK)
    tri = (t[None, :] >= t[:, None]) if upper else (t[None, :] <= t[:, None])
    return jnp.asarray(same & tri, dtype=BF16)


def _hgrn(qh, gf, kf, gb, kb, vh, batch, seq_len, th):
    nt = seq_len // th
    view = lambda t: t.reshape(batch, seq_len, HG_WIDTH)
    fwd = lambda b, i: (b, i, 0)
    bwd = lambda b, i: (b, nt - 1 - i, 0)
    const = lambda b, i: (0, 0)
    tile_f = pl.BlockSpec((None, th, HG_WIDTH), fwd)
    tile_b = pl.BlockSpec((None, th, HG_WIDTH), bwd)
    tri = pl.BlockSpec((th, th), const)
    n_slots = 2 * (th // HG_CHUNK) * HG_HEADS
    return pl.pallas_call(
        functools.partial(_hgrn_kernel, th=th),
        grid=(batch, nt),
        in_specs=[tri, tri, tile_f, tile_f, tile_f, tile_f, tile_b, tile_b, tile_b, tile_b],
        out_specs=[tile_f, tile_b],
        out_shape=[jax.ShapeDtypeStruct((batch, seq_len, HG_WIDTH), BF16)] * 2,
        scratch_shapes=[
            pltpu.VMEM((HG_HEADS, HG_DK, HG_DK), F32),
            pltpu.VMEM((HG_HEADS, HG_DK, HG_DK), F32),
            pltpu.VMEM((th, HG_WIDTH), F32),
            pltpu.VMEM((th, HG_WIDTH), F32),
            pltpu.VMEM((n_slots, HG_CHUNK, HG_CHUNK), F32),
            pltpu.VMEM((th, HG_WIDTH), F32),
        ],
        compiler_params=pltpu.CompilerParams(
            dimension_semantics=("parallel", "arbitrary"), vmem_limit_bytes=VMEM_LIMIT),
    )(_block_triangular(th, False), _block_triangular(th, True),
      view(qh), view(gf), view(kf), view(vh), view(qh), view(gb), view(kb), view(vh))


def _out_ffn_kernel(x_ref, o1_ref, o2_ref, o3_ref, s1_ref, s2_ref, s3_ref, hf_ref, hb_ref, gs_ref,
                    expand_ref, gh_ref, wout_ref, gpm_ref, gpf_ref, gpo_ref, wg_ref, wu_ref, wd_ref,
                    y_ref):
    lses = (s1_ref[...], s2_ref[...], s3_ref[...])
    top = jnp.maximum(jnp.maximum(lses[0], lses[1]), lses[2])
    es = [jnp.exp(s - top) for s in lses]
    den = es[0] + es[1] + es[2]
    att = None
    for e, o_ref in zip(es, (o1_ref, o2_ref, o3_ref)):
        w = jnp.dot((e / den).astype(BF16), expand_ref[...], preferred_element_type=F32)
        term = w * jnp.concatenate([o_ref[s] for s in range(ATT_WIDTH // LANES)], axis=1)
        att = term if att is None else att + term

    o = hf_ref[...].astype(F32) + hb_ref[...].astype(F32)
    hg = jnp.concatenate(
        [_rms(o[:, h * HG_DK:(h + 1) * HG_DK]) * gh_ref[...] for h in range(HG_HEADS)], axis=1)
    hg = hg * gs_ref[...].astype(F32)

    mix_in = jnp.concatenate([att, hg], axis=1).astype(BF16)
    mix = jnp.dot(mix_in, wout_ref[...], preferred_element_type=F32)
    x1 = x_ref[...] + _rms(mix) * gpm_ref[...]

    h2 = (_rms(x1) * gpf_ref[...]).astype(BF16)
    gate = jnp.dot(h2, wg_ref[...], preferred_element_type=F32)
    up = jnp.dot(h2, wu_ref[...], preferred_element_type=F32)
    act = (_silu(gate) * up).astype(BF16)
    ff = jnp.dot(act, wd_ref[...], preferred_element_type=F32)
    y_ref[...] = x1 + _rms(ff) * gpo_ref[...]


def _out_ffn(x, att_o, att_s, hf, hb, gs, g_hnorm, w_out, g_pm, g_pf, g_po, w_gate, w_up, w_down, tm):
    batch, seq_len, _ = x.shape
    lane_head = np.arange(LANES)[:, None]
    col_head = np.arange(ATT_WIDTH)[None, :] // ATT_HEAD_DIM
    expand = jnp.asarray(lane_head == col_head, dtype=BF16)
    const = lambda b, i: (0, 0)
    tile = lambda w: pl.BlockSpec((None, tm, w), lambda b, i: (b, i, 0))
    slab_tile = pl.BlockSpec((None, ATT_WIDTH // LANES, tm, LANES), lambda b, i: (b, 0, i, 0))
    whole = lambda a: pl.BlockSpec(a.shape, const, pipeline_mode=pl.Buffered(1))
    consts = (expand, g_hnorm, w_out, g_pm, g_pf, g_po, w_gate, w_up, w_down)
    return pl.pallas_call(
        _out_ffn_kernel,
        grid=(batch, seq_len // tm),
        in_specs=[tile(D_MODEL)] + [slab_tile] * 3 + [tile(LANES)] * 3 + [tile(HG_WIDTH)] * 3
                 + [whole(a) for a in consts],
        out_specs=tile(D_MODEL),
        out_shape=jax.ShapeDtypeStruct((batch, seq_len, D_MODEL), F32),
        compiler_params=pltpu.CompilerParams(
            dimension_semantics=("parallel", "parallel"), vmem_limit_bytes=VMEM_LIMIT),
    )(x, *att_o, *att_s, hf, hb, gs, *consts)


def _layer(x, w_in, w_out, lb_fwd, lb_bwd, g_hnorm, g_pre_mix, g_post_mix, g_pre_ffn, g_post_ffn,
           w_gate, w_up, w_down, *, tm_in=512, th=256, tm_out=256):
    batch, seq_len, _ = x.shape
    x2 = x.reshape(batch * seq_len, D_MODEL)
    outs = _in_proj(x2, g_pre_mix, w_in, _rope_tables(seq_len), lb_fwd, lb_bwd, seq_len, tm_in)
    n_att = 3 * len(DILATED_BRANCHES)
    qh, gf, kf, gb, kb, vh, gs = outs[n_att:]
    att = [_attn_branch(*outs[3 * b:3 * b + 3], batch, seq_len, dil)
           for b, (_, dil) in enumerate(DILATED_BRANCHES)]
    hf, hb = _hgrn(qh, gf, kf, gb, kb, vh, batch, seq_len, th)
    return _out_ffn(x, [o for o, _ in att], [s for _, s in att], hf, hb,
                    gs.reshape(batch, seq_len, HG_WIDTH), g_hnorm, w_out,
                    g_post_mix, g_pre_ffn, g_post_ffn, w_gate, w_up, w_down, tm_out)


def kernel(x_prompt, x_sample, w_in, w_out, lb_fwd, lb_bwd, g_hgrn_norm, g_pre_mix, g_post_mix,
           g_pre_ffn, g_post_ffn, w_gate, w_up, w_down):
    assert w_in.shape[0] == 1, "one layer"
    assert all(w // (2 * d) == ATT_HALF for w, d in DILATED_BRANCHES)
    params = (w_in[0].astype(BF16), w_out[0].astype(BF16), lb_fwd, lb_bwd, g_hgrn_norm,
              g_pre_mix, g_post_mix, g_pre_ffn, g_post_ffn,
              w_gate[0].astype(BF16), w_up[0].astype(BF16), w_down[0].astype(BF16))
    return _layer(x_prompt, *params), _layer(x_sample, *params)
```

```python
import functools
import math

import jax
import jax.numpy as jnp
import numpy as np
from jax import lax
from jax.experimental import pallas as pl
from jax.experimental.pallas import tpu as pltpu

F32 = jnp.float32
BF16 = jnp.bfloat16

D_MODEL = 1024
ATT_HEADS = 8
ATT_HEAD_DIM = 64
ATT_WIDTH = ATT_HEADS * ATT_HEAD_DIM
DILATED_BRANCHES = ((128, 1), (512, 4), (2048, 16))
ROT_DIM = ATT_HEAD_DIM // 4
ROPE_THETA = 500000.0
HG_HEADS = 4
HG_DK = 128
HG_WIDTH = HG_HEADS * HG_DK
HG_CHUNK = 64
GROUP_W = 512
N_GROUPS = 8
NORM_EPS = 1e-6
NEG_FILL = -1e30
LOG2E = math.log2(math.e)
LN2 = math.log(2.0)

LANES = 128
SUBLANES = 8
ATT_HALF = 64
ATT_TQ = 128
ATT_TK = ATT_TQ + 2 * ATT_HALF
ATT_QUERIES_PER_STEP = 1024
ATT_POS_PER_STEP = (1024, 4096)
SAFE_DECAY_LOG = 80.0
VMEM_LIMIT = 56 * 1024 * 1024


def _sigmoid_pair(z):
    e = jnp.exp(-jnp.abs(z))
    big = 1.0 / (1.0 + e)
    small = e * big
    pos = z >= 0
    return jnp.where(pos, big, small), jnp.where(pos, small, big)


def _silu(z):
    s, _ = _sigmoid_pair(z)
    return z * s


def _rms(x):
    return x * lax.rsqrt(jnp.mean(x * x, axis=-1, keepdims=True) + NORM_EPS)


def _in_proj_kernel(x_ref, g_ref, w_ref, cos_ref, sina_ref, sinb_ref, lbf_ref, lbb_ref, *refs, tm):
    n_br = len(DILATED_BRANCHES)
    att_refs = refs[:3 * n_br]
    qh_ref, gf_ref, kf_ref, gb_ref, kb_ref, vh_ref, gs_ref = refs[3 * n_br:3 * n_br + 7]
    stages = refs[3 * n_br + 7:]
    h = (_rms(x_ref[...]) * g_ref[...]).astype(BF16)
    slabs = GROUP_W // LANES

    def emit_dilated(t, which):
        assert DILATED_BRANCHES[0][1] == 1
        att_refs[which][...] = t.astype(BF16)
        for s in range(slabs):
            stages[0][s] = t[:, s * LANES:(s + 1) * LANES]
        for b in range(1, n_br):
            dil, prev = DILATED_BRANCHES[b][1], DILATED_BRANCHES[b - 1][1]
            step = dil // prev
            out = att_refs[3 * b + which]
            for r in range(dil):
                r_prev, r_new = r % prev, r // prev
                for s in range(slabs):
                    piece = stages[b - 1][r_prev * slabs + s, pl.ds(r_new, tm // dil, stride=step), :]
                    if b + 1 < n_br:
                        stages[b][r * slabs + s] = piece
                    lane0 = r * GROUP_W + s * LANES
                    out[:, lane0:lane0 + LANES] = piece.astype(BF16)

    def proj(j):
        return jnp.dot(h, w_ref[:, j * GROUP_W:(j + 1) * GROUP_W], preferred_element_type=F32)

    reps = GROUP_W // LANES
    cos = jnp.concatenate([cos_ref[...]] * reps, axis=1)
    sina = jnp.concatenate([sina_ref[...]] * reps, axis=1)
    sinb = jnp.concatenate([sinb_ref[...]] * reps, axis=1)
    half = ROT_DIM // 2

    def rope(t):
        return t * cos + pltpu.roll(t, GROUP_W - half, 1) * sina + pltpu.roll(t, half, 1) * sinb

    def lower_bound(lb_ref):
        a = lb_ref[...]
        e = jnp.exp(a - jnp.max(a, axis=0, keepdims=True))
        return e[0:1, :] / jnp.sum(e, axis=0, keepdims=True)

    def gates(z, lb, g_out, k_out):
        s_pos, s_neg = _sigmoid_pair(z)
        g_out[...] = jnp.log(lb + (1.0 - lb) * s_pos)
        k_out[...] = ((1.0 - lb) * s_neg).astype(BF16)

    emit_dilated(rope(proj(0)) * (ATT_HEAD_DIM ** -0.5 * LOG2E), 0)
    emit_dilated(rope(proj(1)), 1)
    emit_dilated(proj(2), 2)
    qh_ref[...] = _silu(proj(3)).astype(BF16)
    gates(proj(4), lower_bound(lbf_ref), gf_ref, kf_ref)
    gates(proj(5), lower_bound(lbb_ref), gb_ref, kb_ref)
    vh_ref[...] = proj(6).astype(BF16)
    gs_ref[...] = _silu(proj(7)).astype(BF16)


def _in_proj(x2, g_pre, w_in, tables, lb_f, lb_b, seq_len, tm):
    n = x2.shape[0]
    n_pos_tiles = seq_len // tm
    row = lambda i: (i, 0)
    const = lambda i: (0, 0)
    pos = lambda i: (i % n_pos_tiles, 0)
    hg_dtypes = (BF16, F32, BF16, F32, BF16, BF16, BF16)
    att_specs, att_shapes = [], []
    for _, dil in DILATED_BRANCHES:
        att_specs += [pl.BlockSpec((tm // dil, dil * GROUP_W), row)] * 3
        att_shapes += [jax.ShapeDtypeStruct((n // dil, dil * GROUP_W), BF16)] * 3
    return pl.pallas_call(
        functools.partial(_in_proj_kernel, tm=tm),
        grid=(n // tm,),
        in_specs=[
            pl.BlockSpec((tm, D_MODEL), row),
            pl.BlockSpec((1, D_MODEL), const),
            pl.BlockSpec((D_MODEL, N_GROUPS * GROUP_W), const, pipeline_mode=pl.Buffered(1)),
            pl.BlockSpec((tm, LANES), pos),
            pl.BlockSpec((tm, LANES), pos),
            pl.BlockSpec((tm, LANES), pos),
            pl.BlockSpec(lb_f.shape, const),
            pl.BlockSpec(lb_b.shape, const),
        ],
        out_specs=att_specs + [pl.BlockSpec((tm, GROUP_W), row)] * len(hg_dtypes),
        out_shape=att_shapes + [jax.ShapeDtypeStruct((n, GROUP_W), dt) for dt in hg_dtypes],
        scratch_shapes=[pltpu.VMEM((dil * GROUP_W // LANES, tm // dil, LANES), F32)
                        for _, dil in DILATED_BRANCHES[:-1]],
        compiler_params=pltpu.CompilerParams(
            dimension_semantics=("parallel",), vmem_limit_bytes=VMEM_LIMIT),
    )(x2, g_pre, w_in, *tables, lb_f, lb_b)


def _rope_tables(seq_len):
    half = ROT_DIM // 2
    dim = np.arange(LANES) % ATT_HEAD_DIM
    first, second = dim < half, (dim >= half) & (dim < ROT_DIM)
    pos = jnp.arange(seq_len, dtype=F32)
    inv_freq = ROPE_THETA ** (-jnp.arange(0, ROT_DIM, 2, dtype=F32) / ROT_DIM)
    ang = pos[:, None] * inv_freq[dim % half][None, :]
    cos, sin = jnp.cos(ang), jnp.sin(ang)
    return (jnp.where(first | second, cos, 1.0), jnp.where(first, -sin, 0.0), jnp.where(second, sin, 0.0))


def _attn_kernel(q_ref, kp_ref, kc_ref, kn_ref, vp_ref, vc_ref, vn_ref, o_ref, st_ref,
                 *, sub_len, tqs, dil, rps):
    n = pl.program_id(1)
    n_sub = tqs // ATT_TQ
    heads_per_group = LANES // ATT_HEAD_DIM
    rows = heads_per_group * ATT_TQ
    qi = lax.broadcasted_iota(jnp.int32, (rows, ATT_TK), 0) % ATT_TQ
    kj = lax.broadcasted_iota(jnp.int32, (rows, ATT_TK), 1)
    band_bias = jnp.where((kj >= qi) & (kj <= qi + 2 * ATT_HALF), 0.0, NEG_FILL)
    key_col = lax.broadcasted_iota(jnp.int32, (1, ATT_TK), 1)
    lane = lax.broadcasted_iota(jnp.int32, (ATT_TQ, LANES), 1)
    first_head = lane < ATT_HEAD_DIM
    ones = jnp.ones((ATT_TK, LANES), BF16)

    def window(prev_ref, cur_ref, next_ref, i, cols):
        lo, hi = i * ATT_TQ - ATT_HALF, (i + 1) * ATT_TQ + ATT_HALF
        parts = []
        if lo < 0:
            parts.append(prev_ref[:, cols])
        parts.append(cur_ref[max(lo, 0):min(hi, tqs), cols])
        if hi > tqs:
            parts.append(next_ref[:, cols])
        return parts[0] if len(parts) == 1 else jnp.concatenate(parts, axis=0)

    for j in range(rps):
        res = pl.program_id(2) * rps + j
        for i in range(n_sub):
            key0 = n * tqs + i * ATT_TQ - ATT_HALF
            bias = band_bias
            if i == 0:
                bias = bias + jnp.where(key_col >= -key0, 0.0, NEG_FILL)
            if i == n_sub - 1:
                bias = bias + jnp.where(key_col < sub_len - key0, 0.0, NEG_FILL)
            if dil == 1:
                rows_out = pl.ds(i * ATT_TQ, ATT_TQ)
            else:
                rows_out = pl.ds(i * ATT_TQ * dil + res, ATT_TQ, stride=dil)
            stats = jnp.zeros((ATT_TQ, LANES), F32)
            for g in range(ATT_WIDTH // LANES):
                cols = slice(j * ATT_WIDTH + g * LANES, j * ATT_WIDTH + (g + 1) * LANES)
                q2 = q_ref[i * ATT_TQ:(i + 1) * ATT_TQ, cols]
                zero = jnp.zeros_like(q2)
                qs = jnp.concatenate(
                    [jnp.where(first_head, q2, zero), jnp.where(first_head, zero, q2)], axis=0)
                kk = window(kp_ref, kc_ref, kn_ref, i, cols)
                s = lax.dot_general(qs, kk, (((1,), (1,)), ((), ())), preferred_element_type=F32) + bias
                m = jnp.max(s, axis=-1, keepdims=True)
                p = jnp.exp2(s - m).astype(BF16)
                vext = jnp.concatenate([window(vp_ref, vc_ref, vn_ref, i, cols), ones], axis=1)
                r = jnp.dot(p, vext, preferred_element_type=F32)
                o0, l0 = r[:ATT_TQ, :LANES], r[:ATT_TQ, LANES:]
                o1, l1 = r[ATT_TQ:, :LANES], r[ATT_TQ:, LANES:]
                o_ref[g, rows_out, :] = jnp.where(first_head, o0 / l0, o1 / l1)
                lse0 = m[:ATT_TQ] * LN2 + jnp.log(l0)
                lse1 = m[ATT_TQ:] * LN2 + jnp.log(l1)
                stats = jnp.where(lane == heads_per_group * g, lse0, stats)
                stats = jnp.where(lane == heads_per_group * g + 1, lse1, stats)
            st_ref[rows_out, :] = stats


def _attn_branch(q, k, v, batch, seq_len, dil):
    sub_len = seq_len // dil
    t_pos = min(seq_len, ATT_POS_PER_STEP[dil > 1])
    tqs = t_pos // dil
    rps = max(1, min(dil, ATT_QUERIES_PER_STEP // tqs))
    assert seq_len % t_pos == 0 and tqs % ATT_TQ == 0 and dil % rps == 0
    halo_per_tile = tqs // ATT_HALF
    n_halo = sub_len // ATT_HALF
    view = lambda t: t.reshape(batch, sub_len, dil * ATT_WIDTH)
    cur = lambda b, n, r: (b, n, r)
    prev = lambda b, n, r: (b, jnp.maximum(n * halo_per_tile - 1, 0), r)
    nxt = lambda b, n, r: (b, jnp.minimum((n + 1) * halo_per_tile, n_halo - 1), r)
    main = pl.BlockSpec((None, tqs, rps * ATT_WIDTH), cur)
    halo_p = pl.BlockSpec((None, ATT_HALF, rps * ATT_WIDTH), prev)
    halo_n = pl.BlockSpec((None, ATT_HALF, rps * ATT_WIDTH), nxt)
    slabs = ATT_WIDTH // LANES
    return pl.pallas_call(
        functools.partial(_attn_kernel, sub_len=sub_len, tqs=tqs, dil=dil, rps=rps),
        grid=(batch, seq_len // t_pos, dil // rps),
        in_specs=[main, halo_p, main, halo_n, halo_p, main, halo_n],
        out_specs=[pl.BlockSpec((None, slabs, t_pos, LANES), lambda b, n, r: (b, 0, n, 0)),
                   pl.BlockSpec((None, t_pos, LANES), lambda b, n, r: (b, n, 0))],
        out_shape=[jax.ShapeDtypeStruct((batch, slabs, seq_len, LANES), F32),
                   jax.ShapeDtypeStruct((batch, seq_len, LANES), F32)],
        compiler_params=pltpu.CompilerParams(
            dimension_semantics=("parallel", "parallel", "arbitrary"), vmem_limit_bytes=VMEM_LIMIT),
    )(view(q), view(k), view(k), view(k), view(v), view(v), view(v))


def _hgrn_kernel(tri_f_ref, tri_b_ref,
                 qf_ref, gf_ref, kf_ref, vf_ref, qb_ref, gb_ref, kb_ref, vb_ref,
                 of_ref, ob_ref,
                 st_f, st_b, b_f, b_b, a_buf, k32, *, th):
    n_chunks = th // HG_CHUNK

    @pl.when(pl.program_id(1) == 0)
    def _():
        st_f[...] = jnp.zeros_like(st_f)
        st_b[...] = jnp.zeros_like(st_b)

    def chunk_sums(tri_ref, g_ref, out):
        g = g_ref[...]
        hi = g.astype(BF16)
        r1 = g - hi.astype(F32)
        mid = r1.astype(BF16)
        lo = (r1 - mid.astype(F32)).astype(BF16)
        tri = tri_ref[...]
        out[...] = (jnp.dot(tri, hi, preferred_element_type=F32)
                    + jnp.dot(tri, mid, preferred_element_type=F32)
                    + jnp.dot(tri, lo, preferred_element_type=F32))

    chunk_sums(tri_f_ref, gf_ref, b_f)
    chunk_sums(tri_b_ref, gb_ref, b_b)

    ti = lax.broadcasted_iota(jnp.int32, (HG_CHUNK, HG_CHUNK), 0)
    si = lax.broadcasted_iota(jnp.int32, (HG_CHUNK, HG_CHUNK), 1)
    dirs = (
        (False, qf_ref, kf_ref, vf_ref, b_f, of_ref, st_f, si <= ti, HG_CHUNK // 2 - 1, HG_CHUNK - 1),
        (True, qb_ref, kb_ref, vb_ref, b_b, ob_ref, st_b, si >= ti, HG_CHUNK // 2, 0),
    )

    safe = jnp.minimum(jnp.min(b_f[...]), jnp.min(b_b[...])) >= -SAFE_DECAY_LOG

    def slot(d, c, h):
        return (d * n_chunks + c) * HG_HEADS + h

    @pl.when(safe)
    def _():
        for d, (_, q_ref, k_ref, _, b, _, _, mask, anchor, _) in enumerate(dirs):
            for c in range(n_chunks):
                rows = slice(c * HG_CHUNK, (c + 1) * HG_CHUNK)
                for h in range(HG_HEADS):
                    cols = slice(h * HG_DK, (h + 1) * HG_DK)
                    bc = b[rows, cols]
                    mid = bc[anchor:anchor + 1, :]
                    qa = (q_ref[rows, cols].astype(F32) * jnp.exp(bc - mid)).astype(BF16)
                    ka = (k_ref[rows, cols].astype(F32) * jnp.exp(mid - bc)).astype(BF16)
                    a = lax.dot_general(qa, ka, (((1,), (1,)), ((), ())), preferred_element_type=F32)
                    a_buf[slot(d, c, h)] = jnp.where(mask, a, 0.0)

    @pl.when(jnp.logical_not(safe))
    def _():
        for d, (_, q_ref, k_ref, _, b, _, _, mask, _, _) in enumerate(dirs):
            k32[...] = k_ref[...].astype(F32)
            for h in range(HG_HEADS):
                cols = slice(h * HG_DK, (h + 1) * HG_DK)

                def per_chunk(c, carry):
                    r0 = pl.multiple_of(c * HG_CHUNK, HG_CHUNK)
                    bc = b[pl.ds(r0, HG_CHUNK), cols]
                    qc = q_ref[pl.ds(r0, HG_CHUNK), cols].astype(F32)

                    def per_key_group(s8, a):
                        k0 = pl.multiple_of(r0 + s8 * SUBLANES, SUBLANES)
                        b8 = b[pl.ds(k0, SUBLANES), cols]
                        k8 = k32[pl.ds(k0, SUBLANES), cols]
                        for j in range(SUBLANES):
                            w = qc * k8[j:j + 1, :] * jnp.exp(jnp.minimum(bc - b8[j:j + 1, :], 0.0))
                            a = jnp.where(si == s8 * SUBLANES + j, jnp.sum(w, axis=-1, keepdims=True), a)
                        return a

                    a = lax.fori_loop(0, HG_CHUNK // SUBLANES, per_key_group,
                                      jnp.zeros((HG_CHUNK, HG_CHUNK), F32))
                    a_buf[slot(d, c, h)] = jnp.where(mask, a, 0.0)
                    return carry

                lax.fori_loop(0, n_chunks, per_chunk, 0)

    for d, (rev, q_ref, k_ref, v_ref, b, o_ref, st, _, _, edge) in enumerate(dirs):
        order = range(n_chunks - 1, -1, -1) if rev else range(n_chunks)
        for c in order:
            rows = slice(c * HG_CHUNK, (c + 1) * HG_CHUNK)
            for h in range(HG_HEADS):
                cols = slice(h * HG_DK, (h + 1) * HG_DK)
                bc = b[rows, cols]
                b_edge = bc[edge:edge + 1, :]
                v = v_ref[rows, cols]
                qi = (q_ref[rows, cols].astype(F32) * jnp.exp(bc)).astype(BF16)
                kd = (k_ref[rows, cols].astype(F32) * jnp.exp(b_edge - bc)).astype(BF16)
                state = st[h]
                o = lax.dot_general(qi, state.astype(BF16), (((1,), (1,)), ((), ())),
                                    preferred_element_type=F32)
                o = o + jnp.dot(a_buf[slot(d, c, h)].astype(BF16), v, preferred_element_type=F32)
                o_ref[rows, cols] = o.astype(o_ref.dtype)
                st[h] = state * jnp.exp(b_edge) + lax.dot_general(
                    v, kd, (((0,), (0,)), ((), ())), preferred_element_type=F32)


def _block_triangular(th, upper):
    t = np.arange(th)
    same = (t[:, None] // HG_CHUNK) == (t[None, :] // HG_CHUNK)
    tri = (t[None, :] >= t[:, None]) if upper else (t[None, :] <= t[:, None])
    return jnp.asarray(same & tri, dtype=BF16)


def _hgrn(qh, gf, kf, gb, kb, vh, batch, seq_len, th):
    nt = seq_len // th
    view = lambda t: t.reshape(batch, seq_len, HG_WIDTH)
    fwd = lambda b, i: (b, i, 0)
    bwd = lambda b, i: (b, nt - 1 - i, 0)
    const = lambda b, i: (0, 0)
    tile_f = pl.BlockSpec((None, th, HG_WIDTH), fwd)
    tile_b = pl.BlockSpec((None, th, HG_WIDTH), bwd)
    tri = pl.BlockSpec((th, th), const)
    n_slots = 2 * (th // HG_CHUNK) * HG_HEADS
    return pl.pallas_call(
        functools.partial(_hgrn_kernel, th=th),
        grid=(batch, nt),
        in_specs=[tri, tri, tile_f, tile_f, tile_f, tile_f, tile_b, tile_b, tile_b, tile_b],
        out_specs=[tile_f, tile_b],
        out_shape=[jax.ShapeDtypeStruct((batch, seq_len, HG_WIDTH), BF16)] * 2,
        scratch_shapes=[
            pltpu.VMEM((HG_HEADS, HG_DK, HG_DK), F32),
            pltpu.VMEM((HG_HEADS, HG_DK, HG_DK), F32),
            pltpu.VMEM((th, HG_WIDTH), F32),
            pltpu.VMEM((th, HG_WIDTH), F32),
            pltpu.VMEM((n_slots, HG_CHUNK, HG_CHUNK), F32),
            pltpu.VMEM((th, HG_WIDTH), F32),
        ],
        compiler_params=pltpu.CompilerParams(
            dimension_semantics=("parallel", "arbitrary"), vmem_limit_bytes=VMEM_LIMIT),
    )(_block_triangular(th, False), _block_triangular(th, True),
      view(qh), view(gf), view(kf), view(vh), view(qh), view(gb), view(kb), view(vh))


def _out_ffn_kernel(x_ref, o1_ref, o2_ref, o3_ref, s1_ref, s2_ref, s3_ref, hf_ref, hb_ref, gs_ref,
                    expand_ref, gh_ref, wout_ref, gpm_ref, gpf_ref, gpo_ref, wg_ref, wu_ref, wd_ref,
                    y_ref):
    lses = (s1_ref[...], s2_ref[...], s3_ref[...])
    top = jnp.maximum(jnp.maximum(lses[0], lses[1]), lses[2])
    es = [jnp.exp(s - top) for s in lses]
    den = es[0] + es[1] + es[2]
    att = None
    for e, o_ref in zip(es, (o1_ref, o2_ref, o3_ref)):
        w = jnp.dot((e / den).astype(BF16), expand_ref[...], preferred_element_type=F32)
        term = w * jnp.concatenate([o_ref[s] for s in range(ATT_WIDTH // LANES)], axis=1)
        att = term if att is None else att + term

    o = hf_ref[...].astype(F32) + hb_ref[...].astype(F32)
    hg = jnp.concatenate(
        [_rms(o[:, h * HG_DK:(h + 1) * HG_DK]) * gh_ref[...] for h in range(HG_HEADS)], axis=1)
    hg = hg * gs_ref[...].astype(F32)

    mix_in = jnp.concatenate([att, hg], axis=1).astype(BF16)
    mix = jnp.dot(mix_in, wout_ref[...], preferred_element_type=F32)
    x1 = x_ref[...] + _rms(mix) * gpm_ref[...]

    h2 = (_rms(x1) * gpf_ref[...]).astype(BF16)
    gate = jnp.dot(h2, wg_ref[...], preferred_element_type=F32)
    up = jnp.dot(h2, wu_ref[...], preferred_element_type=F32)
    act = (_silu(gate) * up).astype(BF16)
    ff = jnp.dot(act, wd_ref[...], preferred_element_type=F32)
    y_ref[...] = x1 + _rms(ff) * gpo_ref[...]


def _out_ffn(x, att_o, att_s, hf, hb, gs, g_hnorm, w_out, g_pm, g_pf, g_po, w_gate, w_up, w_down, tm):
    batch, seq_len, _ = x.shape
    lane_head = np.arange(LANES)[:, None]
    col_head = np.arange(ATT_WIDTH)[None, :] // ATT_HEAD_DIM
    expand = jnp.asarray(lane_head == col_head, dtype=BF16)
    const = lambda b, i: (0, 0)
    tile = lambda w: pl.BlockSpec((None, tm, w), lambda b, i: (b, i, 0))
    slab_tile = pl.BlockSpec((None, ATT_WIDTH // LANES, tm, LANES), lambda b, i: (b, 0, i, 0))
    whole = lambda a: pl.BlockSpec(a.shape, const, pipeline_mode=pl.Buffered(1))
    consts = (expand, g_hnorm, w_out, g_pm, g_pf, g_po, w_gate, w_up, w_down)
    return pl.pallas_call(
        _out_ffn_kernel,
        grid=(batch, seq_len // tm),
        in_specs=[tile(D_MODEL)] + [slab_tile] * 3 + [tile(LANES)] * 3 + [tile(HG_WIDTH)] * 3
                 + [whole(a) for a in consts],
        out_specs=tile(D_MODEL),
        out_shape=jax.ShapeDtypeStruct((batch, seq_len, D_MODEL), F32),
        compiler_params=pltpu.CompilerParams(
            dimension_semantics=("parallel", "parallel"), vmem_limit_bytes=VMEM_LIMIT),
    )(x, *att_o, *att_s, hf, hb, gs, *consts)


def _layer(x, tables, w_in, w_out, lb_fwd, lb_bwd, g_hnorm, g_pre_mix, g_post_mix, g_pre_ffn, g_post_ffn,
           w_gate, w_up, w_down, *, tm_in=512, th=256, tm_out=512):
    batch, seq_len, _ = x.shape
    x2 = x.reshape(batch * seq_len, D_MODEL)
    outs = _in_proj(x2, g_pre_mix, w_in, tables, lb_fwd, lb_bwd, seq_len, tm_in)
    n_att = 3 * len(DILATED_BRANCHES)
    qh, gf, kf, gb, kb, vh, gs = outs[n_att:]
    att = [_attn_branch(*outs[3 * b:3 * b + 3], batch, seq_len, dil)
           for b, (_, dil) in enumerate(DILATED_BRANCHES)]
    hf, hb = _hgrn(qh, gf, kf, gb, kb, vh, batch, seq_len, th)
    return _out_ffn(x, [o for o, _ in att], [s for _, s in att], hf, hb,
                    gs.reshape(batch, seq_len, HG_WIDTH), g_hnorm, w_out,
                    g_post_mix, g_pre_ffn, g_post_ffn, w_gate, w_up, w_down, tm_out)


def kernel(x_prompt, x_sample, w_in, w_out, lb_fwd, lb_bwd, g_hgrn_norm, g_pre_mix, g_post_mix,
           g_pre_ffn, g_post_ffn, w_gate, w_up, w_down):
    assert w_in.shape[0] == 1, "one layer"
    assert all(w // (2 * d) == ATT_HALF for w, d in DILATED_BRANCHES)
    params = (w_in[0].astype(BF16), w_out[0].astype(BF16), lb_fwd, lb_bwd, g_hgrn_norm,
              g_pre_mix, g_post_mix, g_pre_ffn, g_post_ffn,
              w_gate[0].astype(BF16), w_up[0].astype(BF16), w_down[0].astype(BF16))
    tables = _rope_tables(max(x_prompt.shape[1], x_sample.shape[1]))
    return _layer(x_prompt, tables, *params), _layer(x_sample, tables, *params)
```

```python
import functools
import math

import jax
import jax.numpy as jnp
import numpy as np
from jax import lax
from jax.experimental import pallas as pl
from jax.experimental.pallas import tpu as pltpu

F32 = jnp.float32
BF16 = jnp.bfloat16

D_MODEL = 1024
ATT_HEADS = 8
ATT_HEAD_DIM = 64
ATT_WIDTH = ATT_HEADS * ATT_HEAD_DIM
DILATED_BRANCHES = ((128, 1), (512, 4), (2048, 16))
ROT_DIM = ATT_HEAD_DIM // 4
ROPE_THETA = 500000.0
HG_HEADS = 4
HG_DK = 128
HG_WIDTH = HG_HEADS * HG_DK
HG_CHUNK = 64
GROUP_W = 512
N_GROUPS = 8
NORM_EPS = 1e-6
NEG_FILL = -1e30
LOG2E = math.log2(math.e)
LN2 = math.log(2.0)

LANES = 128
SUBLANES = 8
ATT_HALF = 64
ATT_TQ = 128
ATT_TK = ATT_TQ + 2 * ATT_HALF
ATT_QUERIES_PER_STEP = 1024
ATT_POS_PER_STEP = (1024, 4096)
SAFE_DECAY_LOG = 80.0
VMEM_LIMIT = 56 * 1024 * 1024


def _sigmoid_pair(z):
    e = jnp.exp(-jnp.abs(z))
    big = 1.0 / (1.0 + e)
    small = e * big
    pos = z >= 0
    return jnp.where(pos, big, small), jnp.where(pos, small, big)


def _silu(z):
    s, _ = _sigmoid_pair(z)
    return z * s


def _rms(x):
    return x * lax.rsqrt(jnp.mean(x * x, axis=-1, keepdims=True) + NORM_EPS)


def _in_proj_kernel(x_ref, g_ref, w_ref, cos_ref, sina_ref, sinb_ref, lbf_ref, lbb_ref, *refs, tm):
    n_br = len(DILATED_BRANCHES)
    att_refs = refs[:3 * n_br]
    qh_ref, gf_ref, kf_ref, gb_ref, kb_ref, vh_ref, gs_ref = refs[3 * n_br:3 * n_br + 7]
    u_scr = refs[3 * n_br + 7]
    stages = refs[3 * n_br + 8:]
    h = (_rms(x_ref[...]) * g_ref[...]).astype(BF16)
    slabs = GROUP_W // LANES

    def emit_dilated(t, which):
        assert DILATED_BRANCHES[0][1] == 1
        att_refs[which][...] = t.astype(BF16)
        for s in range(slabs):
            stages[0][s] = t[:, s * LANES:(s + 1) * LANES]
        for b in range(1, n_br):
            dil, prev = DILATED_BRANCHES[b][1], DILATED_BRANCHES[b - 1][1]
            step = dil // prev
            out = att_refs[3 * b + which]
            for r in range(dil):
                r_prev, r_new = r % prev, r // prev
                for s in range(slabs):
                    piece = stages[b - 1][r_prev * slabs + s, pl.ds(r_new, tm // dil, stride=step), :]
                    if b + 1 < n_br:
                        stages[b][r * slabs + s] = piece
                    lane0 = r * GROUP_W + s * LANES
                    out[:, lane0:lane0 + LANES] = piece.astype(BF16)

    def proj(j):
        return jnp.dot(h, w_ref[:, j * GROUP_W:(j + 1) * GROUP_W], preferred_element_type=F32)

    reps = GROUP_W // LANES
    cos = jnp.concatenate([cos_ref[...]] * reps, axis=1)
    sina = jnp.concatenate([sina_ref[...]] * reps, axis=1)
    sinb = jnp.concatenate([sinb_ref[...]] * reps, axis=1)
    half = ROT_DIM // 2

    def rope(t):
        return t * cos + pltpu.roll(t, GROUP_W - half, 1) * sina + pltpu.roll(t, half, 1) * sinb

    def lower_bound(lb_ref):
        a = lb_ref[...]
        e = jnp.exp(a - jnp.max(a, axis=0, keepdims=True))
        return e[0:1, :] / jnp.sum(e, axis=0, keepdims=True)

    def gates(z, lb, g_out, k_out):
        s_pos, s_neg = _sigmoid_pair(z)
        g_out[...] = jnp.log(lb + (1.0 - lb) * s_pos)
        k_out[...] = ((1.0 - lb) * s_neg).astype(BF16)

    base = jnp.minimum(pl.program_id(0), 0)

    def store_silu(out):
        def epilogue(t):
            out[...] = _silu(t).astype(BF16)
        return epilogue

    def store_plain(t):
        vh_ref[...] = t.astype(BF16)

    epilogues = (
        lambda t: emit_dilated(rope(t) * (ATT_HEAD_DIM ** -0.5 * LOG2E), 0),
        lambda t: emit_dilated(rope(t), 1),
        lambda t: emit_dilated(t, 2),
        store_silu(qh_ref),
        lambda t: gates(t, lower_bound(lbf_ref), gf_ref, kf_ref),
        lambda t: gates(t, lower_bound(lbb_ref), gb_ref, kb_ref),
        store_plain,
        store_silu(gs_ref),
    )
    order = (0, 1, 2, 3, 4, 5, 7, 6)
    u_scr[base] = proj(order[0])
    for i, j in enumerate(order):
        if i + 1 < N_GROUPS:
            u_scr[base + (i + 1) % 2] = proj(order[i + 1])
        epilogues[j](u_scr[base + i % 2])


def _in_proj(x2, g_pre, w_in, tables, lb_f, lb_b, seq_len, tm):
    n = x2.shape[0]
    n_pos_tiles = seq_len // tm
    row = lambda i: (i, 0)
    const = lambda i: (0, 0)
    pos = lambda i: (i % n_pos_tiles, 0)
    hg_dtypes = (BF16, F32, BF16, F32, BF16, BF16, BF16)
    att_specs, att_shapes = [], []
    for _, dil in DILATED_BRANCHES:
        att_specs += [pl.BlockSpec((tm // dil, dil * GROUP_W), row)] * 3
        att_shapes += [jax.ShapeDtypeStruct((n // dil, dil * GROUP_W), BF16)] * 3
    return pl.pallas_call(
        functools.partial(_in_proj_kernel, tm=tm),
        grid=(n // tm,),
        in_specs=[
            pl.BlockSpec((tm, D_MODEL), row),
            pl.BlockSpec((1, D_MODEL), const),
            pl.BlockSpec((D_MODEL, N_GROUPS * GROUP_W), const, pipeline_mode=pl.Buffered(1)),
            pl.BlockSpec((tm, LANES), pos),
            pl.BlockSpec((tm, LANES), pos),
            pl.BlockSpec((tm, LANES), pos),
            pl.BlockSpec(lb_f.shape, const),
            pl.BlockSpec(lb_b.shape, const),
        ],
        out_specs=att_specs + [pl.BlockSpec((tm, GROUP_W), row)] * len(hg_dtypes),
        out_shape=att_shapes + [jax.ShapeDtypeStruct((n, GROUP_W), dt) for dt in hg_dtypes],
        scratch_shapes=[pltpu.VMEM((2, tm, GROUP_W), F32)]
                       + [pltpu.VMEM((dil * GROUP_W // LANES, tm // dil, LANES), F32)
                          for _, dil in DILATED_BRANCHES[:-1]],
        compiler_params=pltpu.CompilerParams(
            dimension_semantics=("parallel",), vmem_limit_bytes=VMEM_LIMIT),
    )(x2, g_pre, w_in, *tables, lb_f, lb_b)


def _rope_tables(seq_len):
    half = ROT_DIM // 2
    dim = np.arange(LANES) % ATT_HEAD_DIM
    first, second = dim < half, (dim >= half) & (dim < ROT_DIM)
    pos = jnp.arange(seq_len, dtype=F32)
    inv_freq = ROPE_THETA ** (-jnp.arange(0, ROT_DIM, 2, dtype=F32) / ROT_DIM)
    ang = pos[:, None] * inv_freq[dim % half][None, :]
    cos, sin = jnp.cos(ang), jnp.sin(ang)
    return (jnp.where(first | second, cos, 1.0), jnp.where(first, -sin, 0.0), jnp.where(second, sin, 0.0))


def _attn_kernel(q_ref, kp_ref, kc_ref, kn_ref, vp_ref, vc_ref, vn_ref, o_ref, st_ref,
                 *, sub_len, tqs, dil, rps):
    n = pl.program_id(1)
    n_sub = tqs // ATT_TQ
    heads_per_group = LANES // ATT_HEAD_DIM
    rows = heads_per_group * ATT_TQ
    qi = lax.broadcasted_iota(jnp.int32, (rows, ATT_TK), 0) % ATT_TQ
    kj = lax.broadcasted_iota(jnp.int32, (rows, ATT_TK), 1)
    band_bias = jnp.where((kj >= qi) & (kj <= qi + 2 * ATT_HALF), 0.0, NEG_FILL)
    key_col = lax.broadcasted_iota(jnp.int32, (1, ATT_TK), 1)
    lane = lax.broadcasted_iota(jnp.int32, (ATT_TQ, LANES), 1)
    first_head = lane < ATT_HEAD_DIM
    ones = jnp.ones((ATT_TK, LANES), BF16)

    def window(prev_ref, cur_ref, next_ref, i, cols):
        lo, hi = i * ATT_TQ - ATT_HALF, (i + 1) * ATT_TQ + ATT_HALF
        parts = []
        if lo < 0:
            parts.append(prev_ref[:, cols])
        parts.append(cur_ref[max(lo, 0):min(hi, tqs), cols])
        if hi > tqs:
            parts.append(next_ref[:, cols])
        return parts[0] if len(parts) == 1 else jnp.concatenate(parts, axis=0)

    for j in range(rps):
        res = pl.program_id(2) * rps + j
        for i in range(n_sub):
            key0 = n * tqs + i * ATT_TQ - ATT_HALF
            bias = band_bias
            if i == 0:
                bias = bias + jnp.where(key_col >= -key0, 0.0, NEG_FILL)
            if i == n_sub - 1:
                bias = bias + jnp.where(key_col < sub_len - key0, 0.0, NEG_FILL)
            if dil == 1:
                rows_out = pl.ds(i * ATT_TQ, ATT_TQ)
            else:
                rows_out = pl.ds(i * ATT_TQ * dil + res, ATT_TQ, stride=dil)
            stats = jnp.zeros((ATT_TQ, LANES), F32)
            for g in range(ATT_WIDTH // LANES):
                cols = slice(j * ATT_WIDTH + g * LANES, j * ATT_WIDTH + (g + 1) * LANES)
                q2 = q_ref[i * ATT_TQ:(i + 1) * ATT_TQ, cols]
                zero = jnp.zeros_like(q2)
                qs = jnp.concatenate(
                    [jnp.where(first_head, q2, zero), jnp.where(first_head, zero, q2)], axis=0)
                kk = window(kp_ref, kc_ref, kn_ref, i, cols)
                s = lax.dot_general(qs, kk, (((1,), (1,)), ((), ())), preferred_element_type=F32) + bias
                m = jnp.max(s, axis=-1, keepdims=True)
                p = jnp.exp2(s - m).astype(BF16)
                vext = jnp.concatenate([window(vp_ref, vc_ref, vn_ref, i, cols), ones], axis=1)
                r = jnp.dot(p, vext, preferred_element_type=F32)
                o0, l0 = r[:ATT_TQ, :LANES], r[:ATT_TQ, LANES:]
                o1, l1 = r[ATT_TQ:, :LANES], r[ATT_TQ:, LANES:]
                o_ref[g, rows_out, :] = jnp.where(first_head, o0 / l0, o1 / l1)
                lse0 = m[:ATT_TQ] * LN2 + jnp.log(l0)
                lse1 = m[ATT_TQ:] * LN2 + jnp.log(l1)
                stats = jnp.where(lane == heads_per_group * g, lse0, stats)
                stats = jnp.where(lane == heads_per_group * g + 1, lse1, stats)
            st_ref[rows_out, :] = stats


def _attn_branch(q, k, v, batch, seq_len, dil):
    sub_len = seq_len // dil
    t_pos = min(seq_len, ATT_POS_PER_STEP[dil > 1])
    tqs = t_pos // dil
    rps = max(1, min(dil, ATT_QUERIES_PER_STEP // tqs))
    assert seq_len % t_pos == 0 and tqs % ATT_TQ == 0 and dil % rps == 0
    halo_per_tile = tqs // ATT_HALF
    n_halo = sub_len // ATT_HALF
    view = lambda t: t.reshape(batch, sub_len, dil * ATT_WIDTH)
    cur = lambda b, n, r: (b, n, r)
    prev = lambda b, n, r: (b, jnp.maximum(n * halo_per_tile - 1, 0), r)
    nxt = lambda b, n, r: (b, jnp.minimum((n + 1) * halo_per_tile, n_halo - 1), r)
    main = pl.BlockSpec((None, tqs, rps * ATT_WIDTH), cur)
    halo_p = pl.BlockSpec((None, ATT_HALF, rps * ATT_WIDTH), prev)
    halo_n = pl.BlockSpec((None, ATT_HALF, rps * ATT_WIDTH), nxt)
    slabs = ATT_WIDTH // LANES
    return pl.pallas_call(
        functools.partial(_attn_kernel, sub_len=sub_len, tqs=tqs, dil=dil, rps=rps),
        grid=(batch, seq_len // t_pos, dil // rps),
        in_specs=[main, halo_p, main, halo_n, halo_p, main, halo_n],
        out_specs=[pl.BlockSpec((None, slabs, t_pos, LANES), lambda b, n, r: (b, 0, n, 0)),
                   pl.BlockSpec((None, t_pos, LANES), lambda b, n, r: (b, n, 0))],
        out_shape=[jax.ShapeDtypeStruct((batch, slabs, seq_len, LANES), F32),
                   jax.ShapeDtypeStruct((batch, seq_len, LANES), F32)],
        compiler_params=pltpu.CompilerParams(
            dimension_semantics=("parallel", "parallel", "arbitrary"), vmem_limit_bytes=VMEM_LIMIT),
    )(view(q), view(k), view(k), view(k), view(v), view(v), view(v))


def _hgrn_kernel(tri_f_ref, tri_b_ref,
                 qf_ref, gf_ref, kf_ref, vf_ref, qb_ref, gb_ref, kb_ref, vb_ref,
                 of_ref, ob_ref,
                 st_f, st_b, b_f, b_b, oi_f, oi_b, k32, *, th):
    n_chunks = th // HG_CHUNK
    chunk_rows = [slice(c * HG_CHUNK, (c + 1) * HG_CHUNK) for c in range(n_chunks)]

    @pl.when(pl.program_id(1) == 0)
    def _():
        st_f[...] = jnp.zeros_like(st_f)
        st_b[...] = jnp.zeros_like(st_b)

    def chunk_sums(tri_ref, g_ref, out):
        g = g_ref[...]
        hi = g.astype(BF16)
        lo = (g - hi.astype(F32)).astype(BF16)
        tri = tri_ref[...].astype(BF16)
        out[...] = (jnp.dot(tri, hi, preferred_element_type=F32)
                    + jnp.dot(tri, lo, preferred_element_type=F32))

    chunk_sums(tri_f_ref, gf_ref, b_f)
    chunk_sums(tri_b_ref, gb_ref, b_b)

    ti = lax.broadcasted_iota(jnp.int32, (HG_CHUNK, HG_CHUNK), 0)
    si = lax.broadcasted_iota(jnp.int32, (HG_CHUNK, HG_CHUNK), 1)
    dirs = (
        (False, qf_ref, kf_ref, vf_ref, b_f, of_ref, st_f, oi_f, tri_f_ref, si <= ti,
         HG_CHUNK // 2 - 1, HG_CHUNK - 1),
        (True, qb_ref, kb_ref, vb_ref, b_b, ob_ref, st_b, oi_b, tri_b_ref, si >= ti,
         HG_CHUNK // 2, 0),
    )
    nt_dims = (((1,), (1,)), ((), ()))
    tn_dims = (((0,), (0,)), ((), ()))

    streams = [(d, h) for d in range(len(dirs)) for h in range(HG_HEADS)]
    work = {}
    for d, h in streams:
        _, q_ref, k_ref, v_ref, b, _, _, _, tri_ref, _, anchor, edge = dirs[d]
        cols = slice(h * HG_DK, (h + 1) * HG_DK)
        bt = b[:, cols]
        mids = [bt[r.start + anchor:r.start + anchor + 1, :] for r in chunk_rows]
        edges = [bt[r.start + edge:r.start + edge + 1, :] for r in chunk_rows]
        edge_rows = jnp.concatenate([jnp.broadcast_to(e, (HG_CHUNK, HG_DK)) for e in edges], axis=0)
        qi32 = q_ref[:, cols].astype(F32) * jnp.exp(bt)
        kd32 = k_ref[:, cols].astype(F32) * jnp.exp(edge_rows - bt)
        qa = jnp.concatenate([qi32[r] * jnp.exp(-m) for r, m in zip(chunk_rows, mids)], axis=0)
        ka = jnp.concatenate(
            [kd32[r] * jnp.exp(m - e) for r, m, e in zip(chunk_rows, mids, edges)], axis=0)
        a = lax.dot_general(qa.astype(BF16), ka.astype(BF16), nt_dims, preferred_element_type=F32)
        pair_ok = tri_ref[...] > 0.5
        work[d, h] = dict(a=jnp.where(pair_ok, a, 0.0).astype(BF16), edges=edges,
                          qi=qi32.astype(BF16), kd=kd32.astype(BF16), v=v_ref[:, cols], cols=cols)
    for key in streams:
        w = work[key]
        w["o_intra"] = jnp.dot(w["a"], w["v"], preferred_element_type=F32)
    for step in range(n_chunks):
        for d, h in streams:
            rev, _, _, _, _, o_ref, st, oi, _, _, _, _ = dirs[d]
            w = work[d, h]
            c = n_chunks - 1 - step if rev else step
            r = chunk_rows[c]
            state = st[h]
            o_inter = lax.dot_general(w["qi"][r], state.astype(BF16), nt_dims, preferred_element_type=F32)
            oi[r, w["cols"]] = o_inter
            o_ref[r, w["cols"]] = (o_inter + w["o_intra"][r]).astype(o_ref.dtype)
            st[h] = state * jnp.exp(w["edges"][c]) + lax.dot_general(
                w["v"][r], w["kd"][r], tn_dims, preferred_element_type=F32)

    safe = jnp.minimum(jnp.min(b_f[...]), jnp.min(b_b[...])) >= -SAFE_DECAY_LOG

    @pl.when(jnp.logical_not(safe))
    def _():
        for _, q_ref, k_ref, v_ref, b, o_ref, _, oi, _, mask, _, _ in dirs:
            k32[...] = k_ref[...].astype(F32)
            for h in range(HG_HEADS):
                cols = slice(h * HG_DK, (h + 1) * HG_DK)

                def per_chunk(c, carry):
                    r0 = pl.multiple_of(c * HG_CHUNK, HG_CHUNK)
                    bc = b[pl.ds(r0, HG_CHUNK), cols]
                    qc = q_ref[pl.ds(r0, HG_CHUNK), cols].astype(F32)

                    def per_key_group(s8, a):
                        k0 = pl.multiple_of(r0 + s8 * SUBLANES, SUBLANES)
                        b8 = b[pl.ds(k0, SUBLANES), cols]
                        k8 = k32[pl.ds(k0, SUBLANES), cols]
                        for j in range(SUBLANES):
                            w = qc * k8[j:j + 1, :] * jnp.exp(jnp.minimum(bc - b8[j:j + 1, :], 0.0))
                            a = jnp.where(si == s8 * SUBLANES + j, jnp.sum(w, axis=-1, keepdims=True), a)
                        return a

                    a = lax.fori_loop(0, HG_CHUNK // SUBLANES, per_key_group,
                                      jnp.zeros((HG_CHUNK, HG_CHUNK), F32))
                    a = jnp.where(mask, a, 0.0).astype(BF16)
                    o = oi[pl.ds(r0, HG_CHUNK), cols] + jnp.dot(
                        a, v_ref[pl.ds(r0, HG_CHUNK), cols], preferred_element_type=F32)
                    o_ref[pl.ds(r0, HG_CHUNK), cols] = o.astype(o_ref.dtype)
                    return carry

                lax.fori_loop(0, n_chunks, per_chunk, 0)


def _block_triangular(th, upper):
    t = np.arange(th)
    same = (t[:, None] // HG_CHUNK) == (t[None, :] // HG_CHUNK)
    tri = (t[None, :] >= t[:, None]) if upper else (t[None, :] <= t[:, None])
    return jnp.asarray(same & tri, dtype=F32)


def _hgrn(qh, gf, kf, gb, kb, vh, batch, seq_len, th):
    nt = seq_len // th
    view = lambda t: t.reshape(batch, seq_len, HG_WIDTH)
    fwd = lambda b, i: (b, i, 0)
    bwd = lambda b, i: (b, nt - 1 - i, 0)
    const = lambda b, i: (0, 0)
    tile_f = pl.BlockSpec((None, th, HG_WIDTH), fwd)
    tile_b = pl.BlockSpec((None, th, HG_WIDTH), bwd)
    tri = pl.BlockSpec((th, th), const)
    tile_f32 = pltpu.VMEM((th, HG_WIDTH), F32)
    return pl.pallas_call(
        functools.partial(_hgrn_kernel, th=th),
        grid=(batch, nt),
        in_specs=[tri, tri, tile_f, tile_f, tile_f, tile_f, tile_b, tile_b, tile_b, tile_b],
        out_specs=[tile_f, tile_b],
        out_shape=[jax.ShapeDtypeStruct((batch, seq_len, HG_WIDTH), BF16)] * 2,
        scratch_shapes=[
            pltpu.VMEM((HG_HEADS, HG_DK, HG_DK), F32),
            pltpu.VMEM((HG_HEADS, HG_DK, HG_DK), F32),
            tile_f32, tile_f32,
            tile_f32, tile_f32,
            tile_f32,
        ],
        compiler_params=pltpu.CompilerParams(
            dimension_semantics=("parallel", "arbitrary"), vmem_limit_bytes=VMEM_LIMIT),
    )(_block_triangular(th, False), _block_triangular(th, True),
      view(qh), view(gf), view(kf), view(vh), view(qh), view(gb), view(kb), view(vh))


def _out_ffn_kernel(x_ref, o1_ref, o2_ref, o3_ref, s1_ref, s2_ref, s3_ref, hf_ref, hb_ref, gs_ref,
                    expand_ref, gh_ref, wout_ref, gpm_ref, gpf_ref, gpo_ref, wg_ref, wu_ref, wd_ref,
                    y_ref):
    def mixer_output(rows):
        lses = (s1_ref[rows, :], s2_ref[rows, :], s3_ref[rows, :])
        top = jnp.maximum(jnp.maximum(lses[0], lses[1]), lses[2])
        es = [jnp.exp(s - top) for s in lses]
        den = es[0] + es[1] + es[2]
        att = None
        for e, o_ref in zip(es, (o1_ref, o2_ref, o3_ref)):
            w = jnp.dot((e / den).astype(BF16), expand_ref[...], preferred_element_type=F32)
            term = w * jnp.concatenate([o_ref[s, rows, :] for s in range(ATT_WIDTH // LANES)], axis=1)
            att = term if att is None else att + term
        o = hf_ref[rows, :].astype(F32) + hb_ref[rows, :].astype(F32)
        hg = jnp.concatenate(
            [_rms(o[:, h * HG_DK:(h + 1) * HG_DK]) * gh_ref[...] for h in range(HG_HEADS)], axis=1)
        hg = hg * gs_ref[rows, :].astype(F32)
        return jnp.concatenate([att, hg], axis=1).astype(BF16)

    tm = x_ref.shape[0]
    halves = [slice(i * (tm // 2), (i + 1) * (tm // 2)) for i in range(2)]
    dot = functools.partial(jnp.dot, preferred_element_type=F32)
    mix_in = [mixer_output(r) for r in halves]
    mix = [dot(m, wout_ref[...]) for m in mix_in]
    x1 = [x_ref[r, :] + _rms(m) * gpm_ref[...] for r, m in zip(halves, mix)]
    h2 = [(_rms(t) * gpf_ref[...]).astype(BF16) for t in x1]
    gate = [dot(t, wg_ref[...]) for t in h2]
    up = [dot(t, wu_ref[...]) for t in h2]
    act = [(_silu(g) * u).astype(BF16) for g, u in zip(gate, up)]
    ff = [dot(t, wd_ref[...]) for t in act]
    for r, t, f in zip(halves, x1, ff):
        y_ref[r, :] = t + _rms(f) * gpo_ref[...]


def _out_ffn(x, att_o, att_s, hf, hb, gs, g_hnorm, w_out, g_pm, g_pf, g_po, w_gate, w_up, w_down, tm):
    batch, seq_len, _ = x.shape
    lane_head = np.arange(LANES)[:, None]
    col_head = np.arange(ATT_WIDTH)[None, :] // ATT_HEAD_DIM
    expand = jnp.asarray(lane_head == col_head, dtype=BF16)
    const = lambda b, i: (0, 0)
    tile = lambda w: pl.BlockSpec((None, tm, w), lambda b, i: (b, i, 0))
    slab_tile = pl.BlockSpec((None, ATT_WIDTH // LANES, tm, LANES), lambda b, i: (b, 0, i, 0))
    whole = lambda a: pl.BlockSpec(a.shape, const, pipeline_mode=pl.Buffered(1))
    consts = (expand, g_hnorm, w_out, g_pm, g_pf, g_po, w_gate, w_up, w_down)
    return pl.pallas_call(
        _out_ffn_kernel,
        grid=(batch, seq_len // tm),
        in_specs=[tile(D_MODEL)] + [slab_tile] * 3 + [tile(LANES)] * 3 + [tile(HG_WIDTH)] * 3
                 + [whole(a) for a in consts],
        out_specs=tile(D_MODEL),
        out_shape=jax.ShapeDtypeStruct((batch, seq_len, D_MODEL), F32),
        compiler_params=pltpu.CompilerParams(
            dimension_semantics=("parallel", "parallel"), vmem_limit_bytes=VMEM_LIMIT),
    )(x, *att_o, *att_s, hf, hb, gs, *consts)


def _layer(x, tables, w_in, w_out, lb_fwd, lb_bwd, g_hnorm, g_pre_mix, g_post_mix, g_pre_ffn, g_post_ffn,
           w_gate, w_up, w_down, *, tm_in=512, th=256, tm_out=512):
    batch, seq_len, _ = x.shape
    x2 = x.reshape(batch * seq_len, D_MODEL)
    outs = _in_proj(x2, g_pre_mix, w_in, tables, lb_fwd, lb_bwd, seq_len, tm_in)
    n_att = 3 * len(DILATED_BRANCHES)
    qh, gf, kf, gb, kb, vh, gs = outs[n_att:]
    att = [_attn_branch(*outs[3 * b:3 * b + 3], batch, seq_len, dil)
           for b, (_, dil) in enumerate(DILATED_BRANCHES)]
    hf, hb = _hgrn(qh, gf, kf, gb, kb, vh, batch, seq_len, th)
    return _out_ffn(x, [o for o, _ in att], [s for _, s in att], hf, hb,
                    gs.reshape(batch, seq_len, HG_WIDTH), g_hnorm, w_out,
                    g_post_mix, g_pre_ffn, g_post_ffn, w_gate, w_up, w_down, tm_out)


def kernel(x_prompt, x_sample, w_in, w_out, lb_fwd, lb_bwd, g_hgrn_norm, g_pre_mix, g_post_mix,
           g_pre_ffn, g_post_ffn, w_gate, w_up, w_down):
    assert w_in.shape[0] == 1, "one layer"
    assert all(w // (2 * d) == ATT_HALF for w, d in DILATED_BRANCHES)
    params = (w_in[0].astype(BF16), w_out[0].astype(BF16), lb_fwd, lb_bwd, g_hgrn_norm,
              g_pre_mix, g_post_mix, g_pre_ffn, g_post_ffn,
              w_gate[0].astype(BF16), w_up[0].astype(BF16), w_down[0].astype(BF16))
    tables = _rope_tables(max(x_prompt.shape[1], x_sample.shape[1]))
    return _layer(x_prompt, tables, *params), _layer(x_sample, tables, *params)
```

```python
import functools
import math

import jax
import jax.numpy as jnp
import numpy as np
from jax import lax
from jax.experimental import pallas as pl
from jax.experimental.pallas import tpu as pltpu

F32 = jnp.float32
BF16 = jnp.bfloat16

D_MODEL = 1024
ATT_HEADS = 8
ATT_HEAD_DIM = 64
ATT_WIDTH = ATT_HEADS * ATT_HEAD_DIM
DILATED_BRANCHES = ((128, 1), (512, 4), (2048, 16))
ROT_DIM = ATT_HEAD_DIM // 4
ROPE_THETA = 500000.0
HG_HEADS = 4
HG_DK = 128
HG_WIDTH = HG_HEADS * HG_DK
HG_CHUNK = 64
GROUP_W = 512
N_GROUPS = 8
NORM_EPS = 1e-6
NEG_FILL = -1e30
LOG2E = math.log2(math.e)

LANES = 128
SUBLANES = 8
ATT_HALF = 64
ATT_TQ = 128
ATT_TK = ATT_TQ + 2 * ATT_HALF
ATT_QUERIES_PER_STEP = 1024
ATT_POS_PER_STEP = (1024, 4096)
SAFE_DECAY_LOG = 80.0
VMEM_LIMIT = 56 * 1024 * 1024


def _sigmoid_pair(z):
    e = jnp.exp(-jnp.abs(z))
    big = 1.0 / (1.0 + e)
    small = e * big
    pos = z >= 0
    return jnp.where(pos, big, small), jnp.where(pos, small, big)


def _silu(z):
    s, _ = _sigmoid_pair(z)
    return z * s


def _rms(x):
    return x * lax.rsqrt(jnp.mean(x * x, axis=-1, keepdims=True) + NORM_EPS)


def _in_proj_kernel(x_ref, g_ref, w_ref, cos_ref, sina_ref, sinb_ref, lbf_ref, lbb_ref, *refs, tm):
    n_br = len(DILATED_BRANCHES)
    att_refs = refs[:3 * n_br]
    qh_ref, gf_ref, kf_ref, gb_ref, kb_ref, vh_ref, gs_ref = refs[3 * n_br:3 * n_br + 7]
    u_scr = refs[3 * n_br + 7]
    stages = refs[3 * n_br + 8:]
    h = (_rms(x_ref[...]) * g_ref[...]).astype(BF16)
    slabs = GROUP_W // LANES

    def emit_dilated(t, which):
        assert DILATED_BRANCHES[0][1] == 1
        att_refs[which][...] = t.astype(BF16)
        for s in range(slabs):
            stages[0][s] = t[:, s * LANES:(s + 1) * LANES]
        for b in range(1, n_br):
            dil, prev = DILATED_BRANCHES[b][1], DILATED_BRANCHES[b - 1][1]
            step = dil // prev
            out = att_refs[3 * b + which]
            for r in range(dil):
                r_prev, r_new = r % prev, r // prev
                for s in range(slabs):
                    piece = stages[b - 1][r_prev * slabs + s, pl.ds(r_new, tm // dil, stride=step), :]
                    if b + 1 < n_br:
                        stages[b][r * slabs + s] = piece
                    lane0 = r * GROUP_W + s * LANES
                    out[:, lane0:lane0 + LANES] = piece.astype(BF16)

    def proj(j):
        return jnp.dot(h, w_ref[:, j * GROUP_W:(j + 1) * GROUP_W], preferred_element_type=F32)

    reps = GROUP_W // LANES
    cos = jnp.concatenate([cos_ref[...]] * reps, axis=1)
    sina = jnp.concatenate([sina_ref[...]] * reps, axis=1)
    sinb = jnp.concatenate([sinb_ref[...]] * reps, axis=1)
    half = ROT_DIM // 2

    def rope(t):
        return t * cos + pltpu.roll(t, GROUP_W - half, 1) * sina + pltpu.roll(t, half, 1) * sinb

    def lower_bound(lb_ref):
        a = lb_ref[...]
        e = jnp.exp(a - jnp.max(a, axis=0, keepdims=True))
        return e[0:1, :] / jnp.sum(e, axis=0, keepdims=True)

    def gates(z, lb, g_out, k_out):
        s_pos, s_neg = _sigmoid_pair(z)
        g_out[...] = jnp.log(lb + (1.0 - lb) * s_pos)
        k_out[...] = ((1.0 - lb) * s_neg).astype(BF16)

    base = jnp.minimum(pl.program_id(0), 0)

    def store_silu(out):
        def epilogue(t):
            out[...] = _silu(t).astype(BF16)
        return epilogue

    def store_plain(t):
        vh_ref[...] = t.astype(BF16)

    epilogues = (
        lambda t: emit_dilated(rope(t) * (ATT_HEAD_DIM ** -0.5 * LOG2E), 0),
        lambda t: emit_dilated(rope(t), 1),
        lambda t: emit_dilated(t, 2),
        store_silu(qh_ref),
        lambda t: gates(t, lower_bound(lbf_ref), gf_ref, kf_ref),
        lambda t: gates(t, lower_bound(lbb_ref), gb_ref, kb_ref),
        store_plain,
        store_silu(gs_ref),
    )
    order = (0, 1, 2, 3, 4, 5, 7, 6)
    u_scr[base] = proj(order[0])
    for i, j in enumerate(order):
        if i + 1 < N_GROUPS:
            u_scr[base + (i + 1) % 2] = proj(order[i + 1])
        epilogues[j](u_scr[base + i % 2])


def _in_proj(x2, g_pre, w_in, tables, lb_f, lb_b, seq_len, tm):
    n = x2.shape[0]
    n_pos_tiles = seq_len // tm
    row = lambda i: (i, 0)
    const = lambda i: (0, 0)
    pos = lambda i: (i % n_pos_tiles, 0)
    hg_dtypes = (BF16, F32, BF16, F32, BF16, BF16, BF16)
    att_specs, att_shapes = [], []
    for _, dil in DILATED_BRANCHES:
        att_specs += [pl.BlockSpec((tm // dil, dil * GROUP_W), row)] * 3
        att_shapes += [jax.ShapeDtypeStruct((n // dil, dil * GROUP_W), BF16)] * 3
    return pl.pallas_call(
        functools.partial(_in_proj_kernel, tm=tm),
        grid=(n // tm,),
        in_specs=[
            pl.BlockSpec((tm, D_MODEL), row),
            pl.BlockSpec((1, D_MODEL), const),
            pl.BlockSpec((D_MODEL, N_GROUPS * GROUP_W), const, pipeline_mode=pl.Buffered(1)),
            pl.BlockSpec((tm, LANES), pos),
            pl.BlockSpec((tm, LANES), pos),
            pl.BlockSpec((tm, LANES), pos),
            pl.BlockSpec(lb_f.shape, const),
            pl.BlockSpec(lb_b.shape, const),
        ],
        out_specs=att_specs + [pl.BlockSpec((tm, GROUP_W), row)] * len(hg_dtypes),
        out_shape=att_shapes + [jax.ShapeDtypeStruct((n, GROUP_W), dt) for dt in hg_dtypes],
        scratch_shapes=[pltpu.VMEM((2, tm, GROUP_W), F32)]
                       + [pltpu.VMEM((dil * GROUP_W // LANES, tm // dil, LANES), F32)
                          for _, dil in DILATED_BRANCHES[:-1]],
        compiler_params=pltpu.CompilerParams(
            dimension_semantics=("parallel",), vmem_limit_bytes=VMEM_LIMIT),
    )(x2, g_pre, w_in, *tables, lb_f, lb_b)


def _rope_tables(seq_len):
    half = ROT_DIM // 2
    dim = np.arange(LANES) % ATT_HEAD_DIM
    first, second = dim < half, (dim >= half) & (dim < ROT_DIM)
    pos = jnp.arange(seq_len, dtype=F32)
    inv_freq = ROPE_THETA ** (-jnp.arange(0, ROT_DIM, 2, dtype=F32) / ROT_DIM)
    ang = pos[:, None] * inv_freq[dim % half][None, :]
    cos, sin = jnp.cos(ang), jnp.sin(ang)
    return (jnp.where(first | second, cos, 1.0), jnp.where(first, -sin, 0.0), jnp.where(second, sin, 0.0))


def _stat_lane(head):
    return (head % 2) * ATT_HEAD_DIM + head


def _attn_kernel(q_ref, kp_ref, kc_ref, kn_ref, vp_ref, vc_ref, vn_ref, o_ref, st_ref,
                 *, sub_len, tqs, dil, rps):
    n = pl.program_id(1)
    n_sub = tqs // ATT_TQ
    heads_per_group = LANES // ATT_HEAD_DIM
    rows = heads_per_group * ATT_TQ
    qi = lax.broadcasted_iota(jnp.int32, (rows, ATT_TK), 0) % ATT_TQ
    kj = lax.broadcasted_iota(jnp.int32, (rows, ATT_TK), 1)
    band_bias = jnp.where((kj >= qi) & (kj <= qi + 2 * ATT_HALF), 0.0, NEG_FILL)
    key_col = lax.broadcasted_iota(jnp.int32, (1, ATT_TK), 1)
    lane = lax.broadcasted_iota(jnp.int32, (ATT_TQ, LANES), 1)
    first_head = lane < ATT_HEAD_DIM
    stat_lane = [(lane == _stat_lane(heads_per_group * g)) | (lane == _stat_lane(heads_per_group * g + 1))
                 for g in range(ATT_WIDTH // LANES)]
    ones = jnp.ones((ATT_TK, LANES), BF16)

    def window(prev_ref, cur_ref, next_ref, i, cols):
        lo, hi = i * ATT_TQ - ATT_HALF, (i + 1) * ATT_TQ + ATT_HALF
        parts = []
        if lo < 0:
            parts.append(prev_ref[:, cols])
        parts.append(cur_ref[max(lo, 0):min(hi, tqs), cols])
        if hi > tqs:
            parts.append(next_ref[:, cols])
        return parts[0] if len(parts) == 1 else jnp.concatenate(parts, axis=0)

    for j in range(rps):
        res = pl.program_id(2) * rps + j
        for i in range(n_sub):
            key0 = n * tqs + i * ATT_TQ - ATT_HALF
            bias = band_bias
            if i == 0:
                bias = bias + jnp.where(key_col >= -key0, 0.0, NEG_FILL)
            if i == n_sub - 1:
                bias = bias + jnp.where(key_col < sub_len - key0, 0.0, NEG_FILL)
            if dil == 1:
                rows_out = pl.ds(i * ATT_TQ, ATT_TQ)
            else:
                rows_out = pl.ds(i * ATT_TQ * dil + res, ATT_TQ, stride=dil)
            stats = jnp.zeros((ATT_TQ, LANES), F32)
            for g in range(ATT_WIDTH // LANES):
                cols = slice(j * ATT_WIDTH + g * LANES, j * ATT_WIDTH + (g + 1) * LANES)
                q2 = q_ref[i * ATT_TQ:(i + 1) * ATT_TQ, cols]
                zero = jnp.zeros_like(q2)
                qs = jnp.concatenate(
                    [jnp.where(first_head, q2, zero), jnp.where(first_head, zero, q2)], axis=0)
                kk = window(kp_ref, kc_ref, kn_ref, i, cols)
                s = lax.dot_general(qs, kk, (((1,), (1,)), ((), ())), preferred_element_type=F32) + bias
                m = jnp.max(s, axis=-1, keepdims=True)
                p = jnp.exp2((s - m).astype(BF16))
                vext = jnp.concatenate([window(vp_ref, vc_ref, vn_ref, i, cols), ones], axis=1)
                r = jnp.dot(p, vext, preferred_element_type=F32)
                o = jnp.where(first_head, r[:ATT_TQ, :LANES], r[ATT_TQ:, :LANES])
                l = jnp.where(first_head, r[:ATT_TQ, LANES:], r[ATT_TQ:, LANES:])
                o_ref[g, rows_out, :] = o / l
                lse2 = jnp.where(first_head, m[:ATT_TQ], m[ATT_TQ:]) + jnp.log2(l)
                stats = jnp.where(stat_lane[g], lse2, stats)
            st_ref[rows_out, :] = stats


def _attn_branch(q, k, v, batch, seq_len, dil):
    sub_len = seq_len // dil
    t_pos = min(seq_len, ATT_POS_PER_STEP[dil > 1])
    tqs = t_pos // dil
    rps = max(1, min(dil, ATT_QUERIES_PER_STEP // tqs))
    assert seq_len % t_pos == 0 and tqs % ATT_TQ == 0 and dil % rps == 0
    halo_per_tile = tqs // ATT_HALF
    n_halo = sub_len // ATT_HALF
    view = lambda t: t.reshape(batch, sub_len, dil * ATT_WIDTH)
    cur = lambda b, n, r: (b, n, r)
    prev = lambda b, n, r: (b, jnp.maximum(n * halo_per_tile - 1, 0), r)
    nxt = lambda b, n, r: (b, jnp.minimum((n + 1) * halo_per_tile, n_halo - 1), r)
    main = pl.BlockSpec((None, tqs, rps * ATT_WIDTH), cur)
    halo_p = pl.BlockSpec((None, ATT_HALF, rps * ATT_WIDTH), prev)
    halo_n = pl.BlockSpec((None, ATT_HALF, rps * ATT_WIDTH), nxt)
    slabs = ATT_WIDTH // LANES
    return pl.pallas_call(
        functools.partial(_attn_kernel, sub_len=sub_len, tqs=tqs, dil=dil, rps=rps),
        grid=(batch, seq_len // t_pos, dil // rps),
        in_specs=[main, halo_p, main, halo_n, halo_p, main, halo_n],
        out_specs=[pl.BlockSpec((None, slabs, t_pos, LANES), lambda b, n, r: (b, 0, n, 0)),
                   pl.BlockSpec((None, t_pos, LANES), lambda b, n, r: (b, n, 0))],
        out_shape=[jax.ShapeDtypeStruct((batch, slabs, seq_len, LANES), F32),
                   jax.ShapeDtypeStruct((batch, seq_len, LANES), F32)],
        compiler_params=pltpu.CompilerParams(
            dimension_semantics=("parallel", "parallel", "arbitrary"), vmem_limit_bytes=VMEM_LIMIT),
    )(view(q), view(k), view(k), view(k), view(v), view(v), view(v))


def _hgrn_kernel(tri_f_ref, tri_b_ref,
                 qf_ref, gf_ref, kf_ref, vf_ref, qb_ref, gb_ref, kb_ref, vb_ref,
                 of_ref, ob_ref,
                 st_f, st_b, b_f, b_b, oi_f, oi_b, k32, *, th):
    n_chunks = th // HG_CHUNK
    chunk_rows = [slice(c * HG_CHUNK, (c + 1) * HG_CHUNK) for c in range(n_chunks)]

    @pl.when(pl.program_id(1) == 0)
    def _():
        st_f[...] = jnp.zeros_like(st_f)
        st_b[...] = jnp.zeros_like(st_b)

    def chunk_sums(tri_ref, g_ref, out):
        g = g_ref[...]
        hi = g.astype(BF16)
        lo = (g - hi.astype(F32)).astype(BF16)
        tri = tri_ref[...].astype(BF16)
        out[...] = (jnp.dot(tri, hi, preferred_element_type=F32)
                    + jnp.dot(tri, lo, preferred_element_type=F32))

    chunk_sums(tri_f_ref, gf_ref, b_f)
    chunk_sums(tri_b_ref, gb_ref, b_b)

    ti = lax.broadcasted_iota(jnp.int32, (HG_CHUNK, HG_CHUNK), 0)
    si = lax.broadcasted_iota(jnp.int32, (HG_CHUNK, HG_CHUNK), 1)
    dirs = (
        (False, qf_ref, kf_ref, vf_ref, b_f, of_ref, st_f, oi_f, tri_f_ref, si <= ti,
         HG_CHUNK // 2 - 1, HG_CHUNK - 1),
        (True, qb_ref, kb_ref, vb_ref, b_b, ob_ref, st_b, oi_b, tri_b_ref, si >= ti,
         HG_CHUNK // 2, 0),
    )
    nt_dims = (((1,), (1,)), ((), ()))
    tn_dims = (((0,), (0,)), ((), ()))

    streams = [(d, h) for d in range(len(dirs)) for h in range(HG_HEADS)]
    work = {}
    for d, h in streams:
        _, q_ref, k_ref, v_ref, b, _, _, _, tri_ref, _, anchor, edge = dirs[d]
        cols = slice(h * HG_DK, (h + 1) * HG_DK)
        bt = b[:, cols]
        mids = [bt[r.start + anchor:r.start + anchor + 1, :] for r in chunk_rows]
        edges = [bt[r.start + edge:r.start + edge + 1, :] for r in chunk_rows]
        edge_rows = jnp.concatenate([jnp.broadcast_to(e, (HG_CHUNK, HG_DK)) for e in edges], axis=0)
        qi32 = q_ref[:, cols].astype(F32) * jnp.exp(bt)
        kd32 = k_ref[:, cols].astype(F32) * jnp.exp(edge_rows - bt)
        qa = jnp.concatenate([qi32[r] * jnp.exp(-m) for r, m in zip(chunk_rows, mids)], axis=0)
        ka = jnp.concatenate(
            [kd32[r] * jnp.exp(m - e) for r, m, e in zip(chunk_rows, mids, edges)], axis=0)
        a = lax.dot_general(qa.astype(BF16), ka.astype(BF16), nt_dims, preferred_element_type=F32)
        pair_ok = tri_ref[...] > 0.5
        work[d, h] = dict(a=jnp.where(pair_ok, a, 0.0).astype(BF16), edges=edges,
                          qi=qi32.astype(BF16), kd=kd32.astype(BF16), v=v_ref[:, cols], cols=cols)
    for key in streams:
        w = work[key]
        w["o_intra"] = jnp.dot(w["a"], w["v"], preferred_element_type=F32)
    for step in range(n_chunks):
        for d, h in streams:
            rev, _, _, _, _, o_ref, st, oi, _, _, _, _ = dirs[d]
            w = work[d, h]
            c = n_chunks - 1 - step if rev else step
            r = chunk_rows[c]
            state = st[h]
            o_inter = lax.dot_general(w["qi"][r], state.astype(BF16), nt_dims, preferred_element_type=F32)
            oi[r, w["cols"]] = o_inter
            o_ref[r, w["cols"]] = (o_inter + w["o_intra"][r]).astype(o_ref.dtype)
            st[h] = state * jnp.exp(w["edges"][c]) + lax.dot_general(
                w["v"][r], w["kd"][r], tn_dims, preferred_element_type=F32)

    safe = jnp.minimum(jnp.min(b_f[...]), jnp.min(b_b[...])) >= -SAFE_DECAY_LOG

    @pl.when(jnp.logical_not(safe))
    def _():
        for _, q_ref, k_ref, v_ref, b, o_ref, _, oi, _, mask, _, _ in dirs:
            k32[...] = k_ref[...].astype(F32)
            for h in range(HG_HEADS):
                cols = slice(h * HG_DK, (h + 1) * HG_DK)

                def per_chunk(c, carry):
                    r0 = pl.multiple_of(c * HG_CHUNK, HG_CHUNK)
                    bc = b[pl.ds(r0, HG_CHUNK), cols]
                    qc = q_ref[pl.ds(r0, HG_CHUNK), cols].astype(F32)

                    def per_key_group(s8, a):
                        k0 = pl.multiple_of(r0 + s8 * SUBLANES, SUBLANES)
                        b8 = b[pl.ds(k0, SUBLANES), cols]
                        k8 = k32[pl.ds(k0, SUBLANES), cols]
                        for j in range(SUBLANES):
                            w = qc * k8[j:j + 1, :] * jnp.exp(jnp.minimum(bc - b8[j:j + 1, :], 0.0))
                            a = jnp.where(si == s8 * SUBLANES + j, jnp.sum(w, axis=-1, keepdims=True), a)
                        return a

                    a = lax.fori_loop(0, HG_CHUNK // SUBLANES, per_key_group,
                                      jnp.zeros((HG_CHUNK, HG_CHUNK), F32))
                    a = jnp.where(mask, a, 0.0).astype(BF16)
                    o = oi[pl.ds(r0, HG_CHUNK), cols] + jnp.dot(
                        a, v_ref[pl.ds(r0, HG_CHUNK), cols], preferred_element_type=F32)
                    o_ref[pl.ds(r0, HG_CHUNK), cols] = o.astype(o_ref.dtype)
                    return carry

                lax.fori_loop(0, n_chunks, per_chunk, 0)


def _block_triangular(th, upper):
    t = np.arange(th)
    same = (t[:, None] // HG_CHUNK) == (t[None, :] // HG_CHUNK)
    tri = (t[None, :] >= t[:, None]) if upper else (t[None, :] <= t[:, None])
    return jnp.asarray(same & tri, dtype=F32)


def _hgrn(qh, gf, kf, gb, kb, vh, batch, seq_len, th):
    nt = seq_len // th
    view = lambda t: t.reshape(batch, seq_len, HG_WIDTH)
    fwd = lambda b, i: (b, i, 0)
    bwd = lambda b, i: (b, nt - 1 - i, 0)
    const = lambda b, i: (0, 0)
    tile_f = pl.BlockSpec((None, th, HG_WIDTH), fwd)
    tile_b = pl.BlockSpec((None, th, HG_WIDTH), bwd)
    tri = pl.BlockSpec((th, th), const)
    tile_f32 = pltpu.VMEM((th, HG_WIDTH), F32)
    return pl.pallas_call(
        functools.partial(_hgrn_kernel, th=th),
        grid=(batch, nt),
        in_specs=[tri, tri, tile_f, tile_f, tile_f, tile_f, tile_b, tile_b, tile_b, tile_b],
        out_specs=[tile_f, tile_b],
        out_shape=[jax.ShapeDtypeStruct((batch, seq_len, HG_WIDTH), BF16)] * 2,
        scratch_shapes=[
            pltpu.VMEM((HG_HEADS, HG_DK, HG_DK), F32),
            pltpu.VMEM((HG_HEADS, HG_DK, HG_DK), F32),
            tile_f32, tile_f32,
            tile_f32, tile_f32,
            tile_f32,
        ],
        compiler_params=pltpu.CompilerParams(
            dimension_semantics=("parallel", "arbitrary"), vmem_limit_bytes=VMEM_LIMIT),
    )(_block_triangular(th, False), _block_triangular(th, True),
      view(qh), view(gf), view(kf), view(vh), view(qh), view(gb), view(kb), view(vh))


def _out_ffn_kernel(x_ref, o1_ref, o2_ref, o3_ref, s1_ref, s2_ref, s3_ref, hf_ref, hb_ref, gs_ref,
                    expand_ref, gh_ref, wout_ref, gpm_ref, gpf_ref, gpo_ref, wg_ref, wu_ref, wd_ref,
                    y_ref):
    def mixer_output(rows):
        lses = (s1_ref[rows, :], s2_ref[rows, :], s3_ref[rows, :])
        top = jnp.maximum(jnp.maximum(lses[0], lses[1]), lses[2])
        es = [jnp.exp2(s - top) for s in lses]
        den = es[0] + es[1] + es[2]
        att = None
        for e, o_ref in zip(es, (o1_ref, o2_ref, o3_ref)):
            w = jnp.dot((e / den).astype(BF16), expand_ref[...], preferred_element_type=F32)
            term = w * jnp.concatenate([o_ref[s, rows, :] for s in range(ATT_WIDTH // LANES)], axis=1)
            att = term if att is None else att + term
        o = hf_ref[rows, :].astype(F32) + hb_ref[rows, :].astype(F32)
        hg = jnp.concatenate(
            [_rms(o[:, h * HG_DK:(h + 1) * HG_DK]) * gh_ref[...] for h in range(HG_HEADS)], axis=1)
        hg = hg * gs_ref[rows, :].astype(F32)
        return jnp.concatenate([att, hg], axis=1).astype(BF16)

    tm = x_ref.shape[0]
    halves = [slice(i * (tm // 2), (i + 1) * (tm // 2)) for i in range(2)]
    dot = functools.partial(jnp.dot, preferred_element_type=F32)
    mix_in = [mixer_output(r) for r in halves]
    mix = [dot(m, wout_ref[...]) for m in mix_in]
    x1 = [x_ref[r, :] + _rms(m) * gpm_ref[...] for r, m in zip(halves, mix)]
    h2 = [(_rms(t) * gpf_ref[...]).astype(BF16) for t in x1]
    gate = [dot(t, wg_ref[...]) for t in h2]
    up = [dot(t, wu_ref[...]) for t in h2]
    act = [(_silu(g) * u).astype(BF16) for g, u in zip(gate, up)]
    ff = [dot(t, wd_ref[...]) for t in act]
    for r, t, f in zip(halves, x1, ff):
        y_ref[r, :] = t + _rms(f) * gpo_ref[...]


def _out_ffn(x, att_o, att_s, hf, hb, gs, g_hnorm, w_out, g_pm, g_pf, g_po, w_gate, w_up, w_down, tm):
    batch, seq_len, _ = x.shape
    col_head = np.arange(ATT_WIDTH) // ATT_HEAD_DIM
    expand = np.zeros((LANES, ATT_WIDTH), np.float32)
    expand[_stat_lane(col_head), np.arange(ATT_WIDTH)] = 1.0
    expand = jnp.asarray(expand, dtype=BF16)
    const = lambda b, i: (0, 0)
    tile = lambda w: pl.BlockSpec((None, tm, w), lambda b, i: (b, i, 0))
    slab_tile = pl.BlockSpec((None, ATT_WIDTH // LANES, tm, LANES), lambda b, i: (b, 0, i, 0))
    whole = lambda a: pl.BlockSpec(a.shape, const, pipeline_mode=pl.Buffered(1))
    consts = (expand, g_hnorm, w_out, g_pm, g_pf, g_po, w_gate, w_up, w_down)
    return pl.pallas_call(
        _out_ffn_kernel,
        grid=(batch, seq_len // tm),
        in_specs=[tile(D_MODEL)] + [slab_tile] * 3 + [tile(LANES)] * 3 + [tile(HG_WIDTH)] * 3
                 + [whole(a) for a in consts],
        out_specs=tile(D_MODEL),
        out_shape=jax.ShapeDtypeStruct((batch, seq_len, D_MODEL), F32),
        compiler_params=pltpu.CompilerParams(
            dimension_semantics=("parallel", "parallel"), vmem_limit_bytes=VMEM_LIMIT),
    )(x, *att_o, *att_s, hf, hb, gs, *consts)


def _layer(x, tables, w_in, w_out, lb_fwd, lb_bwd, g_hnorm, g_pre_mix, g_post_mix, g_pre_ffn, g_post_ffn,
           w_gate, w_up, w_down, *, tm_in=512, th=256, tm_out=512):
    batch, seq_len, _ = x.shape
    x2 = x.reshape(batch * seq_len, D_MODEL)
    outs = _in_proj(x2, g_pre_mix, w_in, tables, lb_fwd, lb_bwd, seq_len, tm_in)
    n_att = 3 * len(DILATED_BRANCHES)
    qh, gf, kf, gb, kb, vh, gs = outs[n_att:]
    att = [_attn_branch(*outs[3 * b:3 * b + 3], batch, seq_len, dil)
           for b, (_, dil) in enumerate(DILATED_BRANCHES)]
    hf, hb = _hgrn(qh, gf, kf, gb, kb, vh, batch, seq_len, th)
    return _out_ffn(x, [o for o, _ in att], [s for _, s in att], hf, hb,
                    gs.reshape(batch, seq_len, HG_WIDTH), g_hnorm, w_out,
                    g_post_mix, g_pre_ffn, g_post_ffn, w_gate, w_up, w_down, tm_out)


def kernel(x_prompt, x_sample, w_in, w_out, lb_fwd, lb_bwd, g_hgrn_norm, g_pre_mix, g_post_mix,
           g_pre_ffn, g_post_ffn, w_gate, w_up, w_down):
    assert w_in.shape[0] == 1, "one layer"
    assert all(w // (2 * d) == ATT_HALF for w, d in DILATED_BRANCHES)
    params = (w_in[0].astype(BF16), w_out[0].astype(BF16), lb_fwd, lb_bwd, g_hgrn_norm,
              g_pre_mix, g_post_mix, g_pre_ffn, g_post_ffn,
              w_gate[0].astype(BF16), w_up[0].astype(BF16), w_down[0].astype(BF16))
    tables = _rope_tables(max(x_prompt.shape[1], x_sample.shape[1]))
    return _layer(x_prompt, tables, *params), _layer(x_sample, tables, *params)
```

```python
import functools
import math

import jax
import jax.numpy as jnp
import numpy as np
from jax import lax
from jax.experimental import pallas as pl
from jax.experimental.pallas import tpu as pltpu

F32 = jnp.float32
BF16 = jnp.bfloat16

D_MODEL = 1024
ATT_HEADS = 8
ATT_HEAD_DIM = 64
ATT_WIDTH = ATT_HEADS * ATT_HEAD_DIM
DILATED_BRANCHES = ((128, 1), (512, 4), (2048, 16))
ROT_DIM = ATT_HEAD_DIM // 4
ROPE_THETA = 500000.0
HG_HEADS = 4
HG_DK = 128
HG_WIDTH = HG_HEADS * HG_DK
HG_CHUNK = 64
GROUP_W = 512
N_GROUPS = 8
NORM_EPS = 1e-6
NEG_FILL = -1e30
LOG2E = math.log2(math.e)

LANES = 128
SUBLANES = 8
ATT_HALF = 64
ATT_TQ = 128
ATT_TK = ATT_TQ + 2 * ATT_HALF
ATT_QUERIES_PER_STEP = 1024
ATT_POS_PER_STEP = (1024, 4096)
SAFE_DECAY_LOG = 80.0
VMEM_LIMIT = 56 * 1024 * 1024


def _sigmoid_pair(z):
    e = jnp.exp(-jnp.abs(z))
    big = 1.0 / (1.0 + e)
    small = e * big
    pos = z >= 0
    return jnp.where(pos, big, small), jnp.where(pos, small, big)


def _silu(z):
    s, _ = _sigmoid_pair(z)
    return z * s


def _rms(x):
    return x * lax.rsqrt(jnp.mean(x * x, axis=-1, keepdims=True) + NORM_EPS)


def _in_proj_kernel(x_ref, g_ref, w_ref, cos_ref, sina_ref, sinb_ref, lbf_ref, lbb_ref, *refs, tm):
    n_br = len(DILATED_BRANCHES)
    att_refs = refs[:3 * n_br]
    qh_ref, gf_ref, gb_ref, vh_ref, gs_ref = refs[3 * n_br:3 * n_br + 5]
    u_scr = refs[3 * n_br + 5]
    stages = refs[3 * n_br + 6:]
    h = (_rms(x_ref[...]) * g_ref[...]).astype(BF16)
    slabs = GROUP_W // LANES

    def emit_dilated(t, which):
        assert DILATED_BRANCHES[0][1] == 1
        att_refs[which][...] = t.astype(BF16)
        for s in range(slabs):
            stages[0][s] = t[:, s * LANES:(s + 1) * LANES]
        for b in range(1, n_br):
            dil, prev = DILATED_BRANCHES[b][1], DILATED_BRANCHES[b - 1][1]
            step = dil // prev
            out = att_refs[3 * b + which]
            for r in range(dil):
                r_prev, r_new = r % prev, r // prev
                for s in range(slabs):
                    piece = stages[b - 1][r_prev * slabs + s, pl.ds(r_new, tm // dil, stride=step), :]
                    if b + 1 < n_br:
                        stages[b][r * slabs + s] = piece
                    lane0 = r * GROUP_W + s * LANES
                    out[:, lane0:lane0 + LANES] = piece.astype(BF16)

    def proj(j):
        return jnp.dot(h, w_ref[:, j * GROUP_W:(j + 1) * GROUP_W], preferred_element_type=F32)

    reps = GROUP_W // LANES
    cos = jnp.concatenate([cos_ref[...]] * reps, axis=1)
    sina = jnp.concatenate([sina_ref[...]] * reps, axis=1)
    sinb = jnp.concatenate([sinb_ref[...]] * reps, axis=1)
    half = ROT_DIM // 2

    def rope(t):
        return t * cos + pltpu.roll(t, GROUP_W - half, 1) * sina + pltpu.roll(t, half, 1) * sinb

    def lower_bound(lb_ref):
        a = lb_ref[...]
        e = jnp.exp(a - jnp.max(a, axis=0, keepdims=True))
        return e[0:1, :] / jnp.sum(e, axis=0, keepdims=True)

    def gates(z, lb, g_out):
        s_pos, _ = _sigmoid_pair(z)
        g_out[...] = jnp.log(lb + (1.0 - lb) * s_pos)

    base = jnp.minimum(pl.program_id(0), 0)

    def store_silu(out):
        def epilogue(t):
            out[...] = _silu(t).astype(BF16)
        return epilogue

    def store_plain(t):
        vh_ref[...] = t.astype(BF16)

    epilogues = (
        lambda t: emit_dilated(rope(t) * (ATT_HEAD_DIM ** -0.5 * LOG2E), 0),
        lambda t: emit_dilated(rope(t), 1),
        lambda t: emit_dilated(t, 2),
        store_silu(qh_ref),
        lambda t: gates(t, lower_bound(lbf_ref), gf_ref),
        lambda t: gates(t, lower_bound(lbb_ref), gb_ref),
        store_plain,
        store_silu(gs_ref),
    )
    order = (0, 1, 2, 3, 4, 5, 7, 6)
    u_scr[base] = proj(order[0])
    for i, j in enumerate(order):
        if i + 1 < N_GROUPS:
            u_scr[base + (i + 1) % 2] = proj(order[i + 1])
        epilogues[j](u_scr[base + i % 2])


def _in_proj(x2, g_pre, w_in, tables, lb_f, lb_b, seq_len, tm):
    n = x2.shape[0]
    n_pos_tiles = seq_len // tm
    row = lambda i: (i, 0)
    const = lambda i: (0, 0)
    pos = lambda i: (i % n_pos_tiles, 0)
    hg_dtypes = (BF16, F32, F32, BF16, BF16)
    att_specs, att_shapes = [], []
    for _, dil in DILATED_BRANCHES:
        att_specs += [pl.BlockSpec((tm // dil, dil * GROUP_W), row)] * 3
        att_shapes += [jax.ShapeDtypeStruct((n // dil, dil * GROUP_W), BF16)] * 3
    return pl.pallas_call(
        functools.partial(_in_proj_kernel, tm=tm),
        grid=(n // tm,),
        in_specs=[
            pl.BlockSpec((tm, D_MODEL), row),
            pl.BlockSpec((1, D_MODEL), const),
            pl.BlockSpec((D_MODEL, N_GROUPS * GROUP_W), const, pipeline_mode=pl.Buffered(1)),
            pl.BlockSpec((tm, LANES), pos),
            pl.BlockSpec((tm, LANES), pos),
            pl.BlockSpec((tm, LANES), pos),
            pl.BlockSpec(lb_f.shape, const),
            pl.BlockSpec(lb_b.shape, const),
        ],
        out_specs=att_specs + [pl.BlockSpec((tm, GROUP_W), row)] * len(hg_dtypes),
        out_shape=att_shapes + [jax.ShapeDtypeStruct((n, GROUP_W), dt) for dt in hg_dtypes],
        scratch_shapes=[pltpu.VMEM((2, tm, GROUP_W), F32)]
                       + [pltpu.VMEM((dil * GROUP_W // LANES, tm // dil, LANES), F32)
                          for _, dil in DILATED_BRANCHES[:-1]],
        compiler_params=pltpu.CompilerParams(
            dimension_semantics=("parallel",), vmem_limit_bytes=VMEM_LIMIT),
    )(x2, g_pre, w_in, *tables, lb_f, lb_b)


def _rope_tables(seq_len):
    half = ROT_DIM // 2
    dim = np.arange(LANES) % ATT_HEAD_DIM
    first, second = dim < half, (dim >= half) & (dim < ROT_DIM)
    pos = jnp.arange(seq_len, dtype=F32)
    inv_freq = ROPE_THETA ** (-jnp.arange(0, ROT_DIM, 2, dtype=F32) / ROT_DIM)
    ang = pos[:, None] * inv_freq[dim % half][None, :]
    cos, sin = jnp.cos(ang), jnp.sin(ang)
    return (jnp.where(first | second, cos, 1.0), jnp.where(first, -sin, 0.0), jnp.where(second, sin, 0.0))


def _stat_lane(head):
    return (head % 2) * ATT_HEAD_DIM + head


def _attn_kernel(q_ref, kp_ref, kc_ref, kn_ref, vp_ref, vc_ref, vn_ref, o_ref, st_ref, *scratch,
                 sub_len, tqs, dil, rps):
    stage = scratch[0] if dil > 1 else None
    n = pl.program_id(1)
    n_sub = tqs // ATT_TQ
    heads_per_group = LANES // ATT_HEAD_DIM
    rows = heads_per_group * ATT_TQ
    qi = lax.broadcasted_iota(jnp.int32, (rows, ATT_TK), 0) % ATT_TQ
    kj = lax.broadcasted_iota(jnp.int32, (rows, ATT_TK), 1)
    band_bias = jnp.where((kj >= qi) & (kj <= qi + 2 * ATT_HALF), 0.0, NEG_FILL)
    key_col = lax.broadcasted_iota(jnp.int32, (1, ATT_TK), 1)
    lane = lax.broadcasted_iota(jnp.int32, (ATT_TQ, LANES), 1)
    first_head = lane < ATT_HEAD_DIM
    stat_lane = [(lane == _stat_lane(heads_per_group * g)) | (lane == _stat_lane(heads_per_group * g + 1))
                 for g in range(ATT_WIDTH // LANES)]
    ones = jnp.ones((ATT_TK, LANES), BF16)

    def window(prev_ref, cur_ref, next_ref, i, cols):
        lo, hi = i * ATT_TQ - ATT_HALF, (i + 1) * ATT_TQ + ATT_HALF
        parts = []
        if lo < 0:
            parts.append(prev_ref[:, cols])
        parts.append(cur_ref[max(lo, 0):min(hi, tqs), cols])
        if hi > tqs:
            parts.append(next_ref[:, cols])
        return parts[0] if len(parts) == 1 else jnp.concatenate(parts, axis=0)

    for j in range(rps):
        res = pl.program_id(2) * rps + j
        for i in range(n_sub):
            key0 = n * tqs + i * ATT_TQ - ATT_HALF
            bias = band_bias
            if i == 0:
                bias = bias + jnp.where(key_col >= -key0, 0.0, NEG_FILL)
            if i == n_sub - 1:
                bias = bias + jnp.where(key_col < sub_len - key0, 0.0, NEG_FILL)
            if dil == 1:
                rows_out = pl.ds(i * ATT_TQ, ATT_TQ)
            else:
                rows_out = pl.ds(i * ATT_TQ * dil + res, ATT_TQ, stride=dil)
            stats = jnp.zeros((ATT_TQ, LANES), F32)
            for g in range(ATT_WIDTH // LANES):
                cols = slice(j * ATT_WIDTH + g * LANES, j * ATT_WIDTH + (g + 1) * LANES)
                q2 = q_ref[i * ATT_TQ:(i + 1) * ATT_TQ, cols]
                zero = jnp.zeros_like(q2)
                qs = jnp.concatenate(
                    [jnp.where(first_head, q2, zero), jnp.where(first_head, zero, q2)], axis=0)
                kk = window(kp_ref, kc_ref, kn_ref, i, cols)
                s = lax.dot_general(qs, kk, (((1,), (1,)), ((), ())), preferred_element_type=F32) + bias
                m = jnp.max(s, axis=-1, keepdims=True)
                p = jnp.exp2((s - m).astype(BF16))
                vext = jnp.concatenate([window(vp_ref, vc_ref, vn_ref, i, cols), ones], axis=1)
                r = jnp.dot(p, vext, preferred_element_type=F32)
                o = jnp.where(first_head, r[:ATT_TQ, :LANES], r[ATT_TQ:, :LANES])
                l = jnp.where(first_head, r[:ATT_TQ, LANES:], r[ATT_TQ:, LANES:])
                if dil == 1:
                    o_ref[rows_out, g * LANES:(g + 1) * LANES] = (o / l).astype(o_ref.dtype)
                else:
                    stage[g, rows_out, :] = o / l
                lse2 = jnp.where(first_head, m[:ATT_TQ], m[ATT_TQ:]) + jnp.log2(l)
                stats = jnp.where(stat_lane[g], lse2, stats)
            st_ref[rows_out, :] = stats

    if dil > 1:
        @pl.when(pl.program_id(2) == pl.num_programs(2) - 1)
        def _():
            for g in range(ATT_WIDTH // LANES):
                o_ref[:, g * LANES:(g + 1) * LANES] = stage[g].astype(o_ref.dtype)


def _attn_branch(q, k, v, batch, seq_len, dil):
    sub_len = seq_len // dil
    t_pos = min(seq_len, ATT_POS_PER_STEP[dil > 1])
    tqs = t_pos // dil
    rps = max(1, min(dil, ATT_QUERIES_PER_STEP // tqs))
    assert seq_len % t_pos == 0 and tqs % ATT_TQ == 0 and dil % rps == 0
    halo_per_tile = tqs // ATT_HALF
    n_halo = sub_len // ATT_HALF
    view = lambda t: t.reshape(batch, sub_len, dil * ATT_WIDTH)
    cur = lambda b, n, r: (b, n, r)
    prev = lambda b, n, r: (b, jnp.maximum(n * halo_per_tile - 1, 0), r)
    nxt = lambda b, n, r: (b, jnp.minimum((n + 1) * halo_per_tile, n_halo - 1), r)
    main = pl.BlockSpec((None, tqs, rps * ATT_WIDTH), cur)
    halo_p = pl.BlockSpec((None, ATT_HALF, rps * ATT_WIDTH), prev)
    halo_n = pl.BlockSpec((None, ATT_HALF, rps * ATT_WIDTH), nxt)
    slabs = ATT_WIDTH // LANES
    return pl.pallas_call(
        functools.partial(_attn_kernel, sub_len=sub_len, tqs=tqs, dil=dil, rps=rps),
        grid=(batch, seq_len // t_pos, dil // rps),
        in_specs=[main, halo_p, main, halo_n, halo_p, main, halo_n],
        out_specs=[pl.BlockSpec((None, t_pos, ATT_WIDTH), lambda b, n, r: (b, n, 0)),
                   pl.BlockSpec((None, t_pos, LANES), lambda b, n, r: (b, n, 0))],
        out_shape=[jax.ShapeDtypeStruct((batch, seq_len, ATT_WIDTH), BF16),
                   jax.ShapeDtypeStruct((batch, seq_len, LANES), F32)],
        scratch_shapes=[pltpu.VMEM((slabs, t_pos, LANES), F32)] if dil > 1 else [],
        compiler_params=pltpu.CompilerParams(
            dimension_semantics=("parallel", "parallel", "arbitrary"), vmem_limit_bytes=VMEM_LIMIT),
    )(view(q), view(k), view(k), view(k), view(v), view(v), view(v))


def _hgrn_kernel(tri_f_ref, tri_b_ref,
                 qf_ref, gf_ref, vf_ref, qb_ref, gb_ref, vb_ref,
                 of_ref, ob_ref,
                 st_f, st_b, b_f, b_b, oi_f, oi_b, k32, *, th):
    n_chunks = th // HG_CHUNK
    chunk_rows = [slice(c * HG_CHUNK, (c + 1) * HG_CHUNK) for c in range(n_chunks)]

    @pl.when(pl.program_id(1) == 0)
    def _():
        st_f[...] = jnp.zeros_like(st_f)
        st_b[...] = jnp.zeros_like(st_b)

    def chunk_sums(tri_ref, g_ref, out):
        g = g_ref[...]
        hi = g.astype(BF16)
        lo = (g - hi.astype(F32)).astype(BF16)
        tri = tri_ref[...].astype(BF16)
        out[...] = (jnp.dot(tri, hi, preferred_element_type=F32)
                    + jnp.dot(tri, lo, preferred_element_type=F32))

    chunk_sums(tri_f_ref, gf_ref, b_f)
    chunk_sums(tri_b_ref, gb_ref, b_b)

    ti = lax.broadcasted_iota(jnp.int32, (HG_CHUNK, HG_CHUNK), 0)
    si = lax.broadcasted_iota(jnp.int32, (HG_CHUNK, HG_CHUNK), 1)
    dirs = (
        (False, qf_ref, gf_ref, vf_ref, b_f, of_ref, st_f, oi_f, tri_f_ref, si <= ti,
         HG_CHUNK // 2 - 1, HG_CHUNK - 1),
        (True, qb_ref, gb_ref, vb_ref, b_b, ob_ref, st_b, oi_b, tri_b_ref, si >= ti,
         HG_CHUNK // 2, 0),
    )
    nt_dims = (((1,), (1,)), ((), ()))
    tn_dims = (((0,), (0,)), ((), ()))

    streams = [(d, h) for d in range(len(dirs)) for h in range(HG_HEADS)]
    work = {}
    for d, h in streams:
        _, q_ref, g_ref, v_ref, b, _, _, _, tri_ref, _, anchor, edge = dirs[d]
        cols = slice(h * HG_DK, (h + 1) * HG_DK)
        bt = b[:, cols]
        mids = [bt[r.start + anchor:r.start + anchor + 1, :] for r in chunk_rows]
        edges = [bt[r.start + edge:r.start + edge + 1, :] for r in chunk_rows]
        edge_rows = jnp.concatenate([jnp.broadcast_to(e, (HG_CHUNK, HG_DK)) for e in edges], axis=0)
        qi32 = q_ref[:, cols].astype(F32) * jnp.exp(bt)
        k32_s = 1.0 - jnp.exp(g_ref[:, cols])
        kd32 = k32_s * jnp.exp(edge_rows - bt)
        qa = jnp.concatenate([qi32[r] * jnp.exp(-m) for r, m in zip(chunk_rows, mids)], axis=0)
        ka = jnp.concatenate(
            [kd32[r] * jnp.exp(m - e) for r, m, e in zip(chunk_rows, mids, edges)], axis=0)
        a = lax.dot_general(qa.astype(BF16), ka.astype(BF16), nt_dims, preferred_element_type=F32)
        pair_ok = tri_ref[...] > 0.5
        work[d, h] = dict(a=jnp.where(pair_ok, a, 0.0).astype(BF16), edges=edges,
                          qi=qi32.astype(BF16), kd=kd32.astype(BF16), v=v_ref[:, cols], cols=cols)
    for key in streams:
        w = work[key]
        w["o_intra"] = jnp.dot(w["a"], w["v"], preferred_element_type=F32)
    for step in range(n_chunks):
        for d, h in streams:
            rev, _, _, _, _, o_ref, st, oi, _, _, _, _ = dirs[d]
            w = work[d, h]
            c = n_chunks - 1 - step if rev else step
            r = chunk_rows[c]
            state = st[h]
            o_inter = lax.dot_general(w["qi"][r], state.astype(BF16), nt_dims, preferred_element_type=F32)
            oi[r, w["cols"]] = o_inter
            o_ref[r, w["cols"]] = (o_inter + w["o_intra"][r]).astype(o_ref.dtype)
            st[h] = state * jnp.exp(w["edges"][c]) + lax.dot_general(
                w["v"][r], w["kd"][r], tn_dims, preferred_element_type=F32)

    safe = jnp.minimum(jnp.min(b_f[...]), jnp.min(b_b[...])) >= -SAFE_DECAY_LOG

    @pl.when(jnp.logical_not(safe))
    def _():
        for _, q_ref, g_ref, v_ref, b, o_ref, _, oi, _, mask, _, _ in dirs:
            k32[...] = 1.0 - jnp.exp(g_ref[...])
            for h in range(HG_HEADS):
                cols = slice(h * HG_DK, (h + 1) * HG_DK)

                def per_chunk(c, carry):
                    r0 = pl.multiple_of(c * HG_CHUNK, HG_CHUNK)
                    bc = b[pl.ds(r0, HG_CHUNK), cols]
                    qc = q_ref[pl.ds(r0, HG_CHUNK), cols].astype(F32)

                    def per_key_group(s8, a):
                        k0 = pl.multiple_of(r0 + s8 * SUBLANES, SUBLANES)
                        b8 = b[pl.ds(k0, SUBLANES), cols]
                        k8 = k32[pl.ds(k0, SUBLANES), cols]
                        for j in range(SUBLANES):
                            w = qc * k8[j:j + 1, :] * jnp.exp(jnp.minimum(bc - b8[j:j + 1, :], 0.0))
                            a = jnp.where(si == s8 * SUBLANES + j, jnp.sum(w, axis=-1, keepdims=True), a)
                        return a

                    a = lax.fori_loop(0, HG_CHUNK // SUBLANES, per_key_group,
                                      jnp.zeros((HG_CHUNK, HG_CHUNK), F32))
                    a = jnp.where(mask, a, 0.0).astype(BF16)
                    o = oi[pl.ds(r0, HG_CHUNK), cols] + jnp.dot(
                        a, v_ref[pl.ds(r0, HG_CHUNK), cols], preferred_element_type=F32)
                    o_ref[pl.ds(r0, HG_CHUNK), cols] = o.astype(o_ref.dtype)
                    return carry

                lax.fori_loop(0, n_chunks, per_chunk, 0)


def _block_triangular(th, upper):
    t = np.arange(th)
    same = (t[:, None] // HG_CHUNK) == (t[None, :] // HG_CHUNK)
    tri = (t[None, :] >= t[:, None]) if upper else (t[None, :] <= t[:, None])
    return jnp.asarray(same & tri, dtype=F32)


def _hgrn(qh, gf, gb, vh, batch, seq_len, th):
    nt = seq_len // th
    view = lambda t: t.reshape(batch, seq_len, HG_WIDTH)
    fwd = lambda b, i: (b, i, 0)
    bwd = lambda b, i: (b, nt - 1 - i, 0)
    const = lambda b, i: (0, 0)
    tile_f = pl.BlockSpec((None, th, HG_WIDTH), fwd)
    tile_b = pl.BlockSpec((None, th, HG_WIDTH), bwd)
    tri = pl.BlockSpec((th, th), const)
    tile_f32 = pltpu.VMEM((th, HG_WIDTH), F32)
    return pl.pallas_call(
        functools.partial(_hgrn_kernel, th=th),
        grid=(batch, nt),
        in_specs=[tri, tri, tile_f, tile_f, tile_f, tile_b, tile_b, tile_b],
        out_specs=[tile_f, tile_b],
        out_shape=[jax.ShapeDtypeStruct((batch, seq_len, HG_WIDTH), BF16)] * 2,
        scratch_shapes=[
            pltpu.VMEM((HG_HEADS, HG_DK, HG_DK), F32),
            pltpu.VMEM((HG_HEADS, HG_DK, HG_DK), F32),
            tile_f32, tile_f32,
            tile_f32, tile_f32,
            tile_f32,
        ],
        compiler_params=pltpu.CompilerParams(
            dimension_semantics=("parallel", "arbitrary"), vmem_limit_bytes=VMEM_LIMIT),
    )(_block_triangular(th, False), _block_triangular(th, True),
      view(qh), view(gf), view(vh), view(qh), view(gb), view(vh))


def _out_ffn_kernel(x_ref, o1_ref, o2_ref, o3_ref, s1_ref, s2_ref, s3_ref, hf_ref, hb_ref, gs_ref,
                    expand_ref, gh_ref, wout_ref, gpm_ref, gpf_ref, gpo_ref, wg_ref, wu_ref, wd_ref,
                    y_ref):
    def mixer_output(rows):
        lses = (s1_ref[rows, :], s2_ref[rows, :], s3_ref[rows, :])
        top = jnp.maximum(jnp.maximum(lses[0], lses[1]), lses[2])
        es = [jnp.exp2(s - top) for s in lses]
        den = es[0] + es[1] + es[2]
        att = None
        for e, o_ref in zip(es, (o1_ref, o2_ref, o3_ref)):
            w = jnp.dot((e / den).astype(BF16), expand_ref[...], preferred_element_type=F32)
            term = w * o_ref[rows, :].astype(F32)
            att = term if att is None else att + term
        o = hf_ref[rows, :].astype(F32) + hb_ref[rows, :].astype(F32)
        hg = jnp.concatenate(
            [_rms(o[:, h * HG_DK:(h + 1) * HG_DK]) * gh_ref[...] for h in range(HG_HEADS)], axis=1)
        hg = hg * gs_ref[rows, :].astype(F32)
        return jnp.concatenate([att, hg], axis=1).astype(BF16)

    tm = x_ref.shape[0]
    halves = [slice(i * (tm // 2), (i + 1) * (tm // 2)) for i in range(2)]
    dot = functools.partial(jnp.dot, preferred_element_type=F32)
    mix_in = [mixer_output(r) for r in halves]
    mix = [dot(m, wout_ref[...]) for m in mix_in]
    x1 = [x_ref[r, :] + _rms(m) * gpm_ref[...] for r, m in zip(halves, mix)]
    h2 = [(_rms(t) * gpf_ref[...]).astype(BF16) for t in x1]
    gate = [dot(t, wg_ref[...]) for t in h2]
    up = [dot(t, wu_ref[...]) for t in h2]
    act = [(_silu(g) * u).astype(BF16) for g, u in zip(gate, up)]
    ff = [dot(t, wd_ref[...]) for t in act]
    for r, t, f in zip(halves, x1, ff):
        y_ref[r, :] = t + _rms(f) * gpo_ref[...]


def _out_ffn(x, att_o, att_s, hf, hb, gs, g_hnorm, w_out, g_pm, g_pf, g_po, w_gate, w_up, w_down, tm):
    batch, seq_len, _ = x.shape
    col_head = np.arange(ATT_WIDTH) // ATT_HEAD_DIM
    expand = np.zeros((LANES, ATT_WIDTH), np.float32)
    expand[_stat_lane(col_head), np.arange(ATT_WIDTH)] = 1.0
    expand = jnp.asarray(expand, dtype=BF16)
    const = lambda b, i: (0, 0)
    tile = lambda w: pl.BlockSpec((None, tm, w), lambda b, i: (b, i, 0))
    whole = lambda a: pl.BlockSpec(a.shape, const, pipeline_mode=pl.Buffered(1))
    consts = (expand, g_hnorm, w_out, g_pm, g_pf, g_po, w_gate, w_up, w_down)
    return pl.pallas_call(
        _out_ffn_kernel,
        grid=(batch, seq_len // tm),
        in_specs=[tile(D_MODEL)] + [tile(ATT_WIDTH)] * 3 + [tile(LANES)] * 3 + [tile(HG_WIDTH)] * 3
                 + [whole(a) for a in consts],
        out_specs=tile(D_MODEL),
        out_shape=jax.ShapeDtypeStruct((batch, seq_len, D_MODEL), F32),
        compiler_params=pltpu.CompilerParams(
            dimension_semantics=("parallel", "parallel"), vmem_limit_bytes=VMEM_LIMIT),
    )(x, *att_o, *att_s, hf, hb, gs, *consts)


def _layer(x, tables, w_in, w_out, lb_fwd, lb_bwd, g_hnorm, g_pre_mix, g_post_mix, g_pre_ffn, g_post_ffn,
           w_gate, w_up, w_down, *, tm_in=512, th=256, tm_out=512):
    batch, seq_len, _ = x.shape
    x2 = x.reshape(batch * seq_len, D_MODEL)
    outs = _in_proj(x2, g_pre_mix, w_in, tables, lb_fwd, lb_bwd, seq_len, tm_in)
    n_att = 3 * len(DILATED_BRANCHES)
    qh, gf, gb, vh, gs = outs[n_att:]
    att = [_attn_branch(*outs[3 * b:3 * b + 3], batch, seq_len, dil)
           for b, (_, dil) in enumerate(DILATED_BRANCHES)]
    hf, hb = _hgrn(qh, gf, gb, vh, batch, seq_len, th)
    return _out_ffn(x, [o for o, _ in att], [s for _, s in att], hf, hb,
                    gs.reshape(batch, seq_len, HG_WIDTH), g_hnorm, w_out,
                    g_post_mix, g_pre_ffn, g_post_ffn, w_gate, w_up, w_down, tm_out)


def kernel(x_prompt, x_sample, w_in, w_out, lb_fwd, lb_bwd, g_hgrn_norm, g_pre_mix, g_post_mix,
           g_pre_ffn, g_post_ffn, w_gate, w_up, w_down):
    assert w_in.shape[0] == 1, "one layer"
    assert all(w // (2 * d) == ATT_HALF for w, d in DILATED_BRANCHES)
    params = (w_in[0].astype(BF16), w_out[0].astype(BF16), lb_fwd, lb_bwd, g_hgrn_norm,
              g_pre_mix, g_post_mix, g_pre_ffn, g_post_ffn,
              w_gate[0].astype(BF16), w_up[0].astype(BF16), w_down[0].astype(BF16))
    tables = _rope_tables(max(x_prompt.shape[1], x_sample.shape[1]))
    return _layer(x_prompt, tables, *params), _layer(x_sample, tables, *params)
```

```python
import functools
import math

import jax
import jax.numpy as jnp
import numpy as np
from jax import lax
from jax.experimental import pallas as pl
from jax.experimental.pallas import tpu as pltpu

F32 = jnp.float32
BF16 = jnp.bfloat16

D_MODEL = 1024
ATT_HEADS = 8
ATT_HEAD_DIM = 64
ATT_WIDTH = ATT_HEADS * ATT_HEAD_DIM
DILATED_BRANCHES = ((128, 1), (512, 4), (2048, 16))
ROT_DIM = ATT_HEAD_DIM // 4
ROPE_THETA = 500000.0
HG_HEADS = 4
HG_DK = 128
HG_WIDTH = HG_HEADS * HG_DK
HG_CHUNK = 64
HG_SUBTILE = 256
GROUP_W = 512
N_GROUPS = 8
NORM_EPS = 1e-6
NEG_FILL = -1e30
LOG2E = math.log2(math.e)

LANES = 128
SUBLANES = 8
ATT_HALF = 64
ATT_TQ = 128
ATT_TK = ATT_TQ + 2 * ATT_HALF
ATT_QUERIES_PER_STEP = 2048
ATT_POS_PER_STEP = (2048, 4096)
SAFE_DECAY_LOG = 80.0
VMEM_LIMIT = 56 * 1024 * 1024


def _sigmoid_pair(z):
    e = jnp.exp(-jnp.abs(z))
    big = 1.0 / (1.0 + e)
    small = e * big
    pos = z >= 0
    return jnp.where(pos, big, small), jnp.where(pos, small, big)


def _silu(z):
    s, _ = _sigmoid_pair(z)
    return z * s


def _rms(x):
    return x * lax.rsqrt(jnp.mean(x * x, axis=-1, keepdims=True) + NORM_EPS)


def _in_proj_kernel(x_ref, g_ref, w_ref, cos_ref, sina_ref, sinb_ref, lbf_ref, lbb_ref, *refs, tm):
    n_br = len(DILATED_BRANCHES)
    att_refs = refs[:3 * n_br]
    qh_ref, gf_ref, kf_ref, gb_ref, kb_ref, vh_ref, gs_ref = refs[3 * n_br:3 * n_br + 7]
    u_scr = refs[3 * n_br + 7]
    stages = refs[3 * n_br + 8:]
    h = (_rms(x_ref[...]) * g_ref[...]).astype(BF16)
    slabs = GROUP_W // LANES

    def emit_dilated(t, which):
        assert DILATED_BRANCHES[0][1] == 1
        att_refs[which][...] = t.astype(BF16)
        for s in range(slabs):
            stages[0][s] = t[:, s * LANES:(s + 1) * LANES]
        for b in range(1, n_br):
            dil, prev = DILATED_BRANCHES[b][1], DILATED_BRANCHES[b - 1][1]
            step = dil // prev
            out = att_refs[3 * b + which]
            for r in range(dil):
                r_prev, r_new = r % prev, r // prev
                for s in range(slabs):
                    piece = stages[b - 1][r_prev * slabs + s, pl.ds(r_new, tm // dil, stride=step), :]
                    if b + 1 < n_br:
                        stages[b][r * slabs + s] = piece
                    lane0 = r * GROUP_W + s * LANES
                    out[:, lane0:lane0 + LANES] = piece.astype(BF16)

    def proj(j):
        return jnp.dot(h, w_ref[:, j * GROUP_W:(j + 1) * GROUP_W], preferred_element_type=F32)

    reps = GROUP_W // LANES
    cos = jnp.concatenate([cos_ref[...]] * reps, axis=1)
    sina = jnp.concatenate([sina_ref[...]] * reps, axis=1)
    sinb = jnp.concatenate([sinb_ref[...]] * reps, axis=1)
    half = ROT_DIM // 2

    def rope(t):
        return t * cos + pltpu.roll(t, GROUP_W - half, 1) * sina + pltpu.roll(t, half, 1) * sinb

    def lower_bound(lb_ref):
        a = lb_ref[...]
        e = jnp.exp(a - jnp.max(a, axis=0, keepdims=True))
        return e[0:1, :] / jnp.sum(e, axis=0, keepdims=True)

    def gates(z, lb, g_out, k_out):
        s_pos, s_neg = _sigmoid_pair(z)
        g_out[...] = jnp.log(lb + (1.0 - lb) * s_pos)
        k_out[...] = ((1.0 - lb) * s_neg).astype(BF16)

    base = jnp.minimum(pl.program_id(0), 0)

    def store_silu(out):
        def epilogue(t):
            out[...] = _silu(t).astype(BF16)
        return epilogue

    def store_plain(t):
        vh_ref[...] = t.astype(BF16)

    epilogues = (
        lambda t: emit_dilated(rope(t) * (ATT_HEAD_DIM ** -0.5 * LOG2E), 0),
        lambda t: emit_dilated(rope(t), 1),
        lambda t: emit_dilated(t, 2),
        store_silu(qh_ref),
        lambda t: gates(t, lower_bound(lbf_ref), gf_ref, kf_ref),
        lambda t: gates(t, lower_bound(lbb_ref), gb_ref, kb_ref),
        store_plain,
        store_silu(gs_ref),
    )
    order = (0, 1, 2, 3, 4, 5, 7, 6)
    u_scr[base] = proj(order[0])
    for i, j in enumerate(order):
        if i + 1 < N_GROUPS:
            u_scr[base + (i + 1) % 2] = proj(order[i + 1])
        epilogues[j](u_scr[base + i % 2])


def _in_proj(x2, g_pre, w_in, tables, lb_f, lb_b, seq_len, tm):
    n = x2.shape[0]
    n_pos_tiles = seq_len // tm
    row = lambda i: (i, 0)
    const = lambda i: (0, 0)
    pos = lambda i: (i % n_pos_tiles, 0)
    hg_dtypes = (BF16, F32, BF16, F32, BF16, BF16, BF16)
    att_specs, att_shapes = [], []
    for _, dil in DILATED_BRANCHES:
        att_specs += [pl.BlockSpec((tm // dil, dil * GROUP_W), row)] * 3
        att_shapes += [jax.ShapeDtypeStruct((n // dil, dil * GROUP_W), BF16)] * 3
    return pl.pallas_call(
        functools.partial(_in_proj_kernel, tm=tm),
        grid=(n // tm,),
        in_specs=[
            pl.BlockSpec((tm, D_MODEL), row),
            pl.BlockSpec((1, D_MODEL), const),
            pl.BlockSpec((D_MODEL, N_GROUPS * GROUP_W), const, pipeline_mode=pl.Buffered(1)),
            pl.BlockSpec((tm, LANES), pos),
            pl.BlockSpec((tm, LANES), pos),
            pl.BlockSpec((tm, LANES), pos),
            pl.BlockSpec(lb_f.shape, const),
            pl.BlockSpec(lb_b.shape, const),
        ],
        out_specs=att_specs + [pl.BlockSpec((tm, GROUP_W), row)] * len(hg_dtypes),
        out_shape=att_shapes + [jax.ShapeDtypeStruct((n, GROUP_W), dt) for dt in hg_dtypes],
        scratch_shapes=[pltpu.VMEM((2, tm, GROUP_W), F32)]
                       + [pltpu.VMEM((dil * GROUP_W // LANES, tm // dil, LANES), F32)
                          for _, dil in DILATED_BRANCHES[:-1]],
        compiler_params=pltpu.CompilerParams(
            dimension_semantics=("parallel",), vmem_limit_bytes=VMEM_LIMIT),
    )(x2, g_pre, w_in, *tables, lb_f, lb_b)


def _rope_tables(seq_len):
    half = ROT_DIM // 2
    dim = np.arange(LANES) % ATT_HEAD_DIM
    first, second = dim < half, (dim >= half) & (dim < ROT_DIM)
    pos = jnp.arange(seq_len, dtype=F32)
    inv_freq = ROPE_THETA ** (-jnp.arange(0, ROT_DIM, 2, dtype=F32) / ROT_DIM)
    ang = pos[:, None] * inv_freq[dim % half][None, :]
    cos, sin = jnp.cos(ang), jnp.sin(ang)
    return (jnp.where(first | second, cos, 1.0), jnp.where(first, -sin, 0.0), jnp.where(second, sin, 0.0))


def _stat_lane(head):
    return (head % 2) * ATT_HEAD_DIM + head


def _attn_kernel(q_ref, kp_ref, kc_ref, kn_ref, vp_ref, vc_ref, vn_ref, o_ref, st_ref, *scratch,
                 sub_len, tqs, dil, rps):
    stage = scratch[0] if dil > 1 else None
    n = pl.program_id(1)
    n_sub = tqs // ATT_TQ
    heads_per_group = LANES // ATT_HEAD_DIM
    rows = heads_per_group * ATT_TQ
    qi = lax.broadcasted_iota(jnp.int32, (rows, ATT_TK), 0) % ATT_TQ
    kj = lax.broadcasted_iota(jnp.int32, (rows, ATT_TK), 1)
    band_bias = jnp.where((kj >= qi) & (kj <= qi + 2 * ATT_HALF), 0.0, NEG_FILL)
    key_col = lax.broadcasted_iota(jnp.int32, (1, ATT_TK), 1)
    lane = lax.broadcasted_iota(jnp.int32, (ATT_TQ, LANES), 1)
    first_head = lane < ATT_HEAD_DIM
    stat_lane = [(lane == _stat_lane(heads_per_group * g)) | (lane == _stat_lane(heads_per_group * g + 1))
                 for g in range(ATT_WIDTH // LANES)]
    ones = jnp.ones((ATT_TK, LANES), BF16)

    def window(prev_ref, cur_ref, next_ref, i, cols):
        lo, hi = i * ATT_TQ - ATT_HALF, (i + 1) * ATT_TQ + ATT_HALF
        parts = []
        if lo < 0:
            parts.append(prev_ref[:, cols])
        parts.append(cur_ref[max(lo, 0):min(hi, tqs), cols])
        if hi > tqs:
            parts.append(next_ref[:, cols])
        return parts[0] if len(parts) == 1 else jnp.concatenate(parts, axis=0)

    for j in range(rps):
        res = pl.program_id(2) * rps + j
        for i in range(n_sub):
            key0 = n * tqs + i * ATT_TQ - ATT_HALF
            bias = band_bias
            if i == 0:
                bias = bias + jnp.where(key_col >= -key0, 0.0, NEG_FILL)
            if i == n_sub - 1:
                bias = bias + jnp.where(key_col < sub_len - key0, 0.0, NEG_FILL)
            if dil == 1:
                rows_out = pl.ds(i * ATT_TQ, ATT_TQ)
            else:
                rows_out = pl.ds(i * ATT_TQ * dil + res, ATT_TQ, stride=dil)
            stats = jnp.zeros((ATT_TQ, LANES), F32)
            for g in range(ATT_WIDTH // LANES):
                cols = slice(j * ATT_WIDTH + g * LANES, j * ATT_WIDTH + (g + 1) * LANES)
                q2 = q_ref[i * ATT_TQ:(i + 1) * ATT_TQ, cols]
                zero = jnp.zeros_like(q2)
                qs = jnp.concatenate(
                    [jnp.where(first_head, q2, zero), jnp.where(first_head, zero, q2)], axis=0)
                kk = window(kp_ref, kc_ref, kn_ref, i, cols)
                s = lax.dot_general(qs, kk, (((1,), (1,)), ((), ())), preferred_element_type=F32) + bias
                m = jnp.max(s, axis=-1, keepdims=True)
                p = jnp.exp2((s - m).astype(BF16))
                vext = jnp.concatenate([window(vp_ref, vc_ref, vn_ref, i, cols), ones], axis=1)
                r = jnp.dot(p, vext, preferred_element_type=F32)
                o = jnp.where(first_head, r[:ATT_TQ, :LANES], r[ATT_TQ:, :LANES])
                l = jnp.where(first_head, r[:ATT_TQ, LANES:], r[ATT_TQ:, LANES:])
                if dil == 1:
                    o_ref[rows_out, g * LANES:(g + 1) * LANES] = (o / l).astype(o_ref.dtype)
                else:
                    stage[g, rows_out, :] = o / l
                lse2 = jnp.where(first_head, m[:ATT_TQ], m[ATT_TQ:]) + jnp.log2(l)
                stats = jnp.where(stat_lane[g], lse2, stats)
            st_ref[rows_out, :] = stats

    if dil > 1:
        @pl.when(pl.program_id(2) == pl.num_programs(2) - 1)
        def _():
            for g in range(ATT_WIDTH // LANES):
                o_ref[:, g * LANES:(g + 1) * LANES] = stage[g].astype(o_ref.dtype)


def _attn_branch(q, k, v, batch, seq_len, dil):
    sub_len = seq_len // dil
    t_pos = min(seq_len, ATT_POS_PER_STEP[dil > 1])
    tqs = t_pos // dil
    rps = max(1, min(dil, ATT_QUERIES_PER_STEP // tqs))
    assert seq_len % t_pos == 0 and tqs % ATT_TQ == 0 and dil % rps == 0
    halo_per_tile = tqs // ATT_HALF
    n_halo = sub_len // ATT_HALF
    view = lambda t: t.reshape(batch, sub_len, dil * ATT_WIDTH)
    cur = lambda b, n, r: (b, n, r)
    prev = lambda b, n, r: (b, jnp.maximum(n * halo_per_tile - 1, 0), r)
    nxt = lambda b, n, r: (b, jnp.minimum((n + 1) * halo_per_tile, n_halo - 1), r)
    main = pl.BlockSpec((None, tqs, rps * ATT_WIDTH), cur)
    halo_p = pl.BlockSpec((None, ATT_HALF, rps * ATT_WIDTH), prev)
    halo_n = pl.BlockSpec((None, ATT_HALF, rps * ATT_WIDTH), nxt)
    slabs = ATT_WIDTH // LANES
    return pl.pallas_call(
        functools.partial(_attn_kernel, sub_len=sub_len, tqs=tqs, dil=dil, rps=rps),
        grid=(batch, seq_len // t_pos, dil // rps),
        in_specs=[main, halo_p, main, halo_n, halo_p, main, halo_n],
        out_specs=[pl.BlockSpec((None, t_pos, ATT_WIDTH), lambda b, n, r: (b, n, 0)),
                   pl.BlockSpec((None, t_pos, LANES), lambda b, n, r: (b, n, 0))],
        out_shape=[jax.ShapeDtypeStruct((batch, seq_len, ATT_WIDTH), BF16),
                   jax.ShapeDtypeStruct((batch, seq_len, LANES), F32)],
        scratch_shapes=[pltpu.VMEM((slabs, t_pos, LANES), F32)] if dil > 1 else [],
        compiler_params=pltpu.CompilerParams(
            dimension_semantics=("parallel", "parallel", "arbitrary"), vmem_limit_bytes=VMEM_LIMIT),
    )(view(q), view(k), view(k), view(k), view(v), view(v), view(v))


def _hgrn_kernel(tri_f_ref, tri_b_ref,
                 qf_ref, gf_ref, kf_ref, vf_ref, qb_ref, gb_ref, kb_ref, vb_ref,
                 of_ref, ob_ref,
                 st_f, st_b, b_f, b_b, oi_f, oi_b, k32, *, th):
    n_sub = th // HG_SUBTILE
    n_chunks = HG_SUBTILE // HG_CHUNK
    local_rows = [slice(c * HG_CHUNK, (c + 1) * HG_CHUNK) for c in range(n_chunks)]

    @pl.when(pl.program_id(1) == 0)
    def _():
        st_f[...] = jnp.zeros_like(st_f)
        st_b[...] = jnp.zeros_like(st_b)

    def chunk_sums(tri_ref, g_ref, out):
        tri = tri_ref[...].astype(BF16)
        for s in range(n_sub):
            rows = slice(s * HG_SUBTILE, (s + 1) * HG_SUBTILE)
            g = g_ref[rows, :]
            hi = g.astype(BF16)
            lo = (g - hi.astype(F32)).astype(BF16)
            out[rows, :] = (jnp.dot(tri, hi, preferred_element_type=F32)
                            + jnp.dot(tri, lo, preferred_element_type=F32))

    chunk_sums(tri_f_ref, gf_ref, b_f)
    chunk_sums(tri_b_ref, gb_ref, b_b)

    ti = lax.broadcasted_iota(jnp.int32, (HG_CHUNK, HG_CHUNK), 0)
    si = lax.broadcasted_iota(jnp.int32, (HG_CHUNK, HG_CHUNK), 1)
    dirs = (
        (False, qf_ref, kf_ref, vf_ref, b_f, of_ref, st_f, oi_f, tri_f_ref, si <= ti,
         HG_CHUNK // 2 - 1, HG_CHUNK - 1),
        (True, qb_ref, kb_ref, vb_ref, b_b, ob_ref, st_b, oi_b, tri_b_ref, si >= ti,
         HG_CHUNK // 2, 0),
    )
    nt_dims = (((1,), (1,)), ((), ()))
    tn_dims = (((0,), (0,)), ((), ()))

    streams = [(d, h) for d in range(len(dirs)) for h in range(HG_HEADS)]
    for sub in range(n_sub):
        work = {}
        for d, h in streams:
            rev, q_ref, k_ref, v_ref, b, _, _, _, tri_ref, _, anchor, edge = dirs[d]
            row0 = (n_sub - 1 - sub if rev else sub) * HG_SUBTILE
            rows = slice(row0, row0 + HG_SUBTILE)
            cols = slice(h * HG_DK, (h + 1) * HG_DK)
            bt = b[rows, cols]
            mids = [bt[r.start + anchor:r.start + anchor + 1, :] for r in local_rows]
            edges = [bt[r.start + edge:r.start + edge + 1, :] for r in local_rows]
            edge_rows = jnp.concatenate([jnp.broadcast_to(e, (HG_CHUNK, HG_DK)) for e in edges], axis=0)
            qi32 = q_ref[rows, cols].astype(F32) * jnp.exp(bt)
            kd32 = k_ref[rows, cols].astype(F32) * jnp.exp(edge_rows - bt)
            qa = jnp.concatenate([qi32[r] * jnp.exp(-m) for r, m in zip(local_rows, mids)], axis=0)
            ka = jnp.concatenate(
                [kd32[r] * jnp.exp(m - e) for r, m, e in zip(local_rows, mids, edges)], axis=0)
            a = lax.dot_general(qa.astype(BF16), ka.astype(BF16), nt_dims, preferred_element_type=F32)
            pair_ok = tri_ref[...] > 0.5
            work[d, h] = dict(a=jnp.where(pair_ok, a, 0.0).astype(BF16), edges=edges, row0=row0,
                              qi=qi32.astype(BF16), kd=kd32.astype(BF16), v=v_ref[rows, cols], cols=cols)
        for key in streams:
            w = work[key]
            w["o_intra"] = jnp.dot(w["a"], w["v"], preferred_element_type=F32)
        for step in range(n_chunks):
            for d, h in streams:
                rev, _, _, _, _, o_ref, st, oi, _, _, _, _ = dirs[d]
                w = work[d, h]
                c = n_chunks - 1 - step if rev else step
                r = local_rows[c]
                out_rows = slice(w["row0"] + r.start, w["row0"] + r.stop)
                state = st[h]
                o_inter = lax.dot_general(w["qi"][r], state.astype(BF16), nt_dims,
                                          preferred_element_type=F32)
                oi[out_rows, w["cols"]] = o_inter
                o_ref[out_rows, w["cols"]] = (o_inter + w["o_intra"][r]).astype(o_ref.dtype)
                st[h] = state * jnp.exp(w["edges"][c]) + lax.dot_general(
                    w["v"][r], w["kd"][r], tn_dims, preferred_element_type=F32)

    safe = jnp.minimum(jnp.min(b_f[...]), jnp.min(b_b[...])) >= -SAFE_DECAY_LOG

    @pl.when(jnp.logical_not(safe))
    def _():
        for _, q_ref, k_ref, v_ref, b, o_ref, _, oi, _, mask, _, _ in dirs:
            k32[...] = k_ref[...].astype(F32)
            for h in range(HG_HEADS):
                cols = slice(h * HG_DK, (h + 1) * HG_DK)

                def per_chunk(c, carry):
                    r0 = pl.multiple_of(c * HG_CHUNK, HG_CHUNK)
                    bc = b[pl.ds(r0, HG_CHUNK), cols]
                    qc = q_ref[pl.ds(r0, HG_CHUNK), cols].astype(F32)

                    def per_key_group(s8, a):
                        k0 = pl.multiple_of(r0 + s8 * SUBLANES, SUBLANES)
                        b8 = b[pl.ds(k0, SUBLANES), cols]
                        k8 = k32[pl.ds(k0, SUBLANES), cols]
                        for j in range(SUBLANES):
                            w = qc * k8[j:j + 1, :] * jnp.exp(jnp.minimum(bc - b8[j:j + 1, :], 0.0))
                            a = jnp.where(si == s8 * SUBLANES + j, jnp.sum(w, axis=-1, keepdims=True), a)
                        return a

                    a = lax.fori_loop(0, HG_CHUNK // SUBLANES, per_key_group,
                                      jnp.zeros((HG_CHUNK, HG_CHUNK), F32))
                    a = jnp.where(mask, a, 0.0).astype(BF16)
                    o = oi[pl.ds(r0, HG_CHUNK), cols] + jnp.dot(
                        a, v_ref[pl.ds(r0, HG_CHUNK), cols], preferred_element_type=F32)
                    o_ref[pl.ds(r0, HG_CHUNK), cols] = o.astype(o_ref.dtype)
                    return carry

                lax.fori_loop(0, th // HG_CHUNK, per_chunk, 0)


def _block_triangular(th, upper):
    t = np.arange(th)
    same = (t[:, None] // HG_CHUNK) == (t[None, :] // HG_CHUNK)
    tri = (t[None, :] >= t[:, None]) if upper else (t[None, :] <= t[:, None])
    return jnp.asarray(same & tri, dtype=F32)


def _hgrn(qh, gf, kf, gb, kb, vh, batch, seq_len, th):
    nt = seq_len // th
    view = lambda t: t.reshape(batch, seq_len, HG_WIDTH)
    fwd = lambda b, i: (b, i, 0)
    bwd = lambda b, i: (b, nt - 1 - i, 0)
    const = lambda b, i: (0, 0)
    tile_f = pl.BlockSpec((None, th, HG_WIDTH), fwd)
    tile_b = pl.BlockSpec((None, th, HG_WIDTH), bwd)
    tri = pl.BlockSpec((HG_SUBTILE, HG_SUBTILE), const)
    tile_f32 = pltpu.VMEM((th, HG_WIDTH), F32)
    return pl.pallas_call(
        functools.partial(_hgrn_kernel, th=th),
        grid=(batch, nt),
        in_specs=[tri, tri, tile_f, tile_f, tile_f, tile_f, tile_b, tile_b, tile_b, tile_b],
        out_specs=[tile_f, tile_b],
        out_shape=[jax.ShapeDtypeStruct((batch, seq_len, HG_WIDTH), BF16)] * 2,
        scratch_shapes=[
            pltpu.VMEM((HG_HEADS, HG_DK, HG_DK), F32),
            pltpu.VMEM((HG_HEADS, HG_DK, HG_DK), F32),
            tile_f32, tile_f32,
            tile_f32, tile_f32,
            tile_f32,
        ],
        compiler_params=pltpu.CompilerParams(
            dimension_semantics=("parallel", "arbitrary"), vmem_limit_bytes=VMEM_LIMIT),
    )(_block_triangular(HG_SUBTILE, False), _block_triangular(HG_SUBTILE, True),
      view(qh), view(gf), view(kf), view(vh), view(qh), view(gb), view(kb), view(vh))


def _out_ffn_kernel(x_ref, o1_ref, o2_ref, o3_ref, s1_ref, s2_ref, s3_ref, hf_ref, hb_ref, gs_ref,
                    expand_ref, gh_ref, wout_ref, gpm_ref, gpf_ref, gpo_ref, wg_ref, wu_ref, wd_ref,
                    y_ref):
    def mixer_output(rows):
        lses = (s1_ref[rows, :], s2_ref[rows, :], s3_ref[rows, :])
        top = jnp.maximum(jnp.maximum(lses[0], lses[1]), lses[2])
        es = [jnp.exp2(s - top) for s in lses]
        den = es[0] + es[1] + es[2]
        att = None
        for e, o_ref in zip(es, (o1_ref, o2_ref, o3_ref)):
            w = jnp.dot((e / den).astype(BF16), expand_ref[...], preferred_element_type=F32)
            term = w * o_ref[rows, :].astype(F32)
            att = term if att is None else att + term
        o = hf_ref[rows, :].astype(F32) + hb_ref[rows, :].astype(F32)
        hg = jnp.concatenate(
            [_rms(o[:, h * HG_DK:(h + 1) * HG_DK]) * gh_ref[...] for h in range(HG_HEADS)], axis=1)
        hg = hg * gs_ref[rows, :].astype(F32)
        return jnp.concatenate([att, hg], axis=1).astype(BF16)

    tm = x_ref.shape[0]
    halves = [slice(i * (tm // 2), (i + 1) * (tm // 2)) for i in range(2)]
    dot = functools.partial(jnp.dot, preferred_element_type=F32)
    mix_in = [mixer_output(r) for r in halves]
    mix = [dot(m, wout_ref[...]) for m in mix_in]
    x1 = [x_ref[r, :] + _rms(m) * gpm_ref[...] for r, m in zip(halves, mix)]
    h2 = [(_rms(t) * gpf_ref[...]).astype(BF16) for t in x1]
    gate = [dot(t, wg_ref[...]) for t in h2]
    up = [dot(t, wu_ref[...]) for t in h2]
    act = [(_silu(g) * u).astype(BF16) for g, u in zip(gate, up)]
    ff = [dot(t, wd_ref[...]) for t in act]
    for r, t, f in zip(halves, x1, ff):
        y_ref[r, :] = t + _rms(f) * gpo_ref[...]


def _out_ffn(x, att_o, att_s, hf, hb, gs, g_hnorm, w_out, g_pm, g_pf, g_po, w_gate, w_up, w_down, tm):
    batch, seq_len, _ = x.shape
    col_head = np.arange(ATT_WIDTH) // ATT_HEAD_DIM
    expand = np.zeros((LANES, ATT_WIDTH), np.float32)
    expand[_stat_lane(col_head), np.arange(ATT_WIDTH)] = 1.0
    expand = jnp.asarray(expand, dtype=BF16)
    const = lambda b, i: (0, 0)
    tile = lambda w: pl.BlockSpec((None, tm, w), lambda b, i: (b, i, 0))
    whole = lambda a: pl.BlockSpec(a.shape, const, pipeline_mode=pl.Buffered(1))
    consts = (expand, g_hnorm, w_out, g_pm, g_pf, g_po, w_gate, w_up, w_down)
    return pl.pallas_call(
        _out_ffn_kernel,
        grid=(batch, seq_len // tm),
        in_specs=[tile(D_MODEL)] + [tile(ATT_WIDTH)] * 3 + [tile(LANES)] * 3 + [tile(HG_WIDTH)] * 3
                 + [whole(a) for a in consts],
        out_specs=tile(D_MODEL),
        out_shape=jax.ShapeDtypeStruct((batch, seq_len, D_MODEL), F32),
        compiler_params=pltpu.CompilerParams(
            dimension_semantics=("parallel", "parallel"), vmem_limit_bytes=VMEM_LIMIT),
    )(x, *att_o, *att_s, hf, hb, gs, *consts)


def _layer(x, tables, w_in, w_out, lb_fwd, lb_bwd, g_hnorm, g_pre_mix, g_post_mix, g_pre_ffn, g_post_ffn,
           w_gate, w_up, w_down, *, tm_in=512, th=512, tm_out=512):
    batch, seq_len, _ = x.shape
    x2 = x.reshape(batch * seq_len, D_MODEL)
    outs = _in_proj(x2, g_pre_mix, w_in, tables, lb_fwd, lb_bwd, seq_len, tm_in)
    n_att = 3 * len(DILATED_BRANCHES)
    qh, gf, kf, gb, kb, vh, gs = outs[n_att:]
    att = [_attn_branch(*outs[3 * b:3 * b + 3], batch, seq_len, dil)
           for b, (_, dil) in enumerate(DILATED_BRANCHES)]
    hf, hb = _hgrn(qh, gf, kf, gb, kb, vh, batch, seq_len, th)
    return _out_ffn(x, [o for o, _ in att], [s for _, s in att], hf, hb,
                    gs.reshape(batch, seq_len, HG_WIDTH), g_hnorm, w_out,
                    g_post_mix, g_pre_ffn, g_post_ffn, w_gate, w_up, w_down, tm_out)


def kernel(x_prompt, x_sample, w_in, w_out, lb_fwd, lb_bwd, g_hgrn_norm, g_pre_mix, g_post_mix,
           g_pre_ffn, g_post_ffn, w_gate, w_up, w_down):
    assert w_in.shape[0] == 1, "one layer"
    assert all(w // (2 * d) == ATT_HALF for w, d in DILATED_BRANCHES)
    params = (w_in[0].astype(BF16), w_out[0].astype(BF16), lb_fwd, lb_bwd, g_hgrn_norm,
              g_pre_mix, g_post_mix, g_pre_ffn, g_post_ffn,
              w_gate[0].astype(BF16), w_up[0].astype(BF16), w_down[0].astype(BF16))
    tables = _rope_tables(max(x_prompt.shape[1], x_sample.shape[1]))
    return _layer(x_prompt, tables, *params), _layer(x_sample, tables, *params)
```

```python
import functools
import math

import jax
import jax.numpy as jnp
import numpy as np
from jax import lax
from jax.experimental import pallas as pl
from jax.experimental.pallas import tpu as pltpu

F32 = jnp.float32
BF16 = jnp.bfloat16

D_MODEL = 1024
ATT_HEADS = 8
ATT_HEAD_DIM = 64
ATT_WIDTH = ATT_HEADS * ATT_HEAD_DIM
DILATED_BRANCHES = ((128, 1), (512, 4), (2048, 16))
ROT_DIM = ATT_HEAD_DIM // 4
ROPE_THETA = 500000.0
HG_HEADS = 4
HG_DK = 128
HG_WIDTH = HG_HEADS * HG_DK
HG_CHUNK = 64
HG_SUBTILE = 256
GROUP_W = 512
N_GROUPS = 8
NORM_EPS = 1e-6
NEG_FILL = -1e30
LOG2E = math.log2(math.e)

LANES = 128
SUBLANES = 8
ATT_HALF = 64
ATT_TQ = 128
ATT_TK = ATT_TQ + 2 * ATT_HALF
ATT_QUERIES_PER_STEP = 2048
ATT_POS_PER_STEP = (4096, 4096)
IN_PROJ_ROWS = 512
HG_TILE_ROWS = 1024
OUT_FFN_ROWS = 512
SAFE_DECAY_LOG = 80.0
VMEM_LIMIT = 56 * 1024 * 1024


def _sigmoid_pair(z):
    e = jnp.exp(-jnp.abs(z))
    big = 1.0 / (1.0 + e)
    small = e * big
    pos = z >= 0
    return jnp.where(pos, big, small), jnp.where(pos, small, big)


def _silu(z):
    s, _ = _sigmoid_pair(z)
    return z * s


def _rms(x):
    return x * lax.rsqrt(jnp.mean(x * x, axis=-1, keepdims=True) + NORM_EPS)


def _in_proj_kernel(x_ref, g_ref, w_ref, rope_base_ref, rope_delta_ref, lbf_ref, lbb_ref, *refs, tm):
    n_br = len(DILATED_BRANCHES)
    att_refs = refs[:3 * n_br]
    qh_ref, gf_ref, kf_ref, gb_ref, kb_ref, vh_ref, gs_ref = refs[3 * n_br:3 * n_br + 7]
    u_scr = refs[3 * n_br + 7]
    stages = refs[3 * n_br + 8:]
    slabs = GROUP_W // LANES
    n_parts = 1
    part = tm // n_parts
    part_rows = [slice(p * part, (p + 1) * part) for p in range(n_parts)]
    h = [(_rms(x_ref[r, :]) * g_ref[...]).astype(BF16) for r in part_rows]

    def emit_dilated(t, which, p):
        assert DILATED_BRANCHES[0][1] == 1
        att_refs[which][part_rows[p], :] = t.astype(BF16)
        for s in range(slabs):
            stages[0][s, part_rows[p], :] = t[:, s * LANES:(s + 1) * LANES]
        for b in range(1, n_br):
            dil, prev = DILATED_BRANCHES[b][1], DILATED_BRANCHES[b - 1][1]
            step = dil // prev
            out = att_refs[3 * b + which]
            rows = slice(p * part // dil, (p + 1) * part // dil)
            for r in range(dil):
                r_prev, r_new = r % prev, r // prev
                for s in range(slabs):
                    piece = stages[b - 1][r_prev * slabs + s,
                                          pl.ds(r_new + step * rows.start, part // dil, stride=step), :]
                    if b + 1 < n_br:
                        stages[b][r * slabs + s, rows, :] = piece
                    lane0 = r * GROUP_W + s * LANES
                    out[rows, lane0:lane0 + LANES] = piece.astype(BF16)

    def proj(j, p):
        return jnp.dot(h[p], w_ref[:, j * GROUP_W:(j + 1) * GROUP_W], preferred_element_type=F32)

    cb, sb = rope_base_ref[0:1, :], rope_base_ref[1:2, :]
    dt = rope_delta_ref
    reps = GROUP_W // LANES
    half = ROT_DIM // 2

    def rope(t, p):
        r = part_rows[p]
        cos = jnp.concatenate([cb * dt[0, r, :] - sb * dt[1, r, :] + dt[2, r, :]] * reps, axis=1)
        sina = jnp.concatenate([sb * dt[3, r, :] + cb * dt[4, r, :]] * reps, axis=1)
        sinb = jnp.concatenate([sb * dt[5, r, :] + cb * dt[6, r, :]] * reps, axis=1)
        return t * cos + pltpu.roll(t, GROUP_W - half, 1) * sina + pltpu.roll(t, half, 1) * sinb

    def lower_bound(lb_ref):
        a = lb_ref[...]
        e = jnp.exp(a - jnp.max(a, axis=0, keepdims=True))
        return e[0:1, :] / jnp.sum(e, axis=0, keepdims=True)

    def gates(z, lb, g_out, k_out, p):
        s_pos, s_neg = _sigmoid_pair(z)
        g_out[part_rows[p], :] = jnp.log(lb + (1.0 - lb) * s_pos)
        k_out[part_rows[p], :] = ((1.0 - lb) * s_neg).astype(BF16)

    def store_silu(out):
        def epilogue(t, p):
            out[part_rows[p], :] = _silu(t).astype(BF16)
        return epilogue

    def store_plain(t, p):
        vh_ref[part_rows[p], :] = t.astype(BF16)

    epilogues = (
        lambda t, p: emit_dilated(rope(t, p) * (ATT_HEAD_DIM ** -0.5 * LOG2E), 0, p),
        lambda t, p: emit_dilated(rope(t, p), 1, p),
        lambda t, p: emit_dilated(t, 2, p),
        store_silu(qh_ref),
        lambda t, p: gates(t, lower_bound(lbf_ref), gf_ref, kf_ref, p),
        lambda t, p: gates(t, lower_bound(lbb_ref), gb_ref, kb_ref, p),
        store_plain,
        store_silu(gs_ref),
    )

    base = jnp.minimum(pl.program_id(0), 0)
    order = (0, 1, 2, 3, 4, 5, 7, 6)
    for p in range(n_parts):
        u_scr[base, part_rows[p], :] = proj(order[0], p)
    for i, j in enumerate(order):
        for p in range(n_parts):
            if i + 1 < N_GROUPS:
                u_scr[base + (i + 1) % 2, part_rows[p], :] = proj(order[i + 1], p)
            epilogues[j](u_scr[base + i % 2, part_rows[p], :], p)


def _in_proj(x2, g_pre, w_in, tables, lb_f, lb_b, seq_len, tm):
    n = x2.shape[0]
    n_pos_tiles = seq_len // tm
    row = lambda i: (i, 0)
    const = lambda i: (0, 0)
    rope_base, rope_delta = tables
    hg_dtypes = (BF16, F32, BF16, F32, BF16, BF16, BF16)
    att_specs, att_shapes = [], []
    for _, dil in DILATED_BRANCHES:
        att_specs += [pl.BlockSpec((tm // dil, dil * GROUP_W), row)] * 3
        att_shapes += [jax.ShapeDtypeStruct((n // dil, dil * GROUP_W), BF16)] * 3
    return pl.pallas_call(
        functools.partial(_in_proj_kernel, tm=tm),
        grid=(n // tm,),
        in_specs=[
            pl.BlockSpec((tm, D_MODEL), row),
            pl.BlockSpec((1, D_MODEL), const),
            pl.BlockSpec((D_MODEL, N_GROUPS * GROUP_W), const, pipeline_mode=pl.Buffered(1)),
            pl.BlockSpec((None, SUBLANES, LANES), lambda i: (i % n_pos_tiles, 0, 0)),
            pl.BlockSpec(rope_delta.shape, lambda i: (0, 0, 0), pipeline_mode=pl.Buffered(1)),
            pl.BlockSpec(lb_f.shape, const),
            pl.BlockSpec(lb_b.shape, const),
        ],
        out_specs=att_specs + [pl.BlockSpec((tm, GROUP_W), row)] * len(hg_dtypes),
        out_shape=att_shapes + [jax.ShapeDtypeStruct((n, GROUP_W), dt) for dt in hg_dtypes],
        scratch_shapes=[pltpu.VMEM((2, tm, GROUP_W), F32)]
                       + [pltpu.VMEM((dil * GROUP_W // LANES, tm // dil, LANES), F32)
                          for _, dil in DILATED_BRANCHES[:-1]],
        compiler_params=pltpu.CompilerParams(
            dimension_semantics=("parallel",), vmem_limit_bytes=VMEM_LIMIT),
    )(x2, g_pre, w_in, rope_base, rope_delta, lb_f, lb_b)


def _rope_tables(seq_len, tm):
    half = ROT_DIM // 2
    dim = np.arange(LANES) % ATT_HEAD_DIM
    first, second = dim < half, (dim >= half) & (dim < ROT_DIM)
    rotary = first | second
    inv_freq = ROPE_THETA ** (-jnp.arange(0, ROT_DIM, 2, dtype=F32) / ROT_DIM)
    freq = inv_freq[dim % half][None, :]
    start = jnp.arange(0, seq_len, tm, dtype=F32)[:, None] * freq
    offset = jnp.arange(tm, dtype=F32)[:, None] * freq
    pad = jnp.zeros((seq_len // tm, SUBLANES - 2, LANES), F32)
    base = jnp.concatenate([jnp.cos(start)[:, None], jnp.sin(start)[:, None], pad], axis=1)
    cd, sd = jnp.cos(offset), jnp.sin(offset)
    zero = jnp.zeros_like(cd)
    delta = jnp.stack([
        jnp.where(rotary, cd, zero), jnp.where(rotary, sd, zero), jnp.where(rotary, zero, 1.0),
        jnp.where(first, -cd, zero), jnp.where(first, -sd, zero),
        jnp.where(second, cd, zero), jnp.where(second, sd, zero)])
    return base, delta


def _stat_lane(head):
    return (head % 2) * ATT_HEAD_DIM + head


def _attn_kernel(q_ref, kp_ref, kc_ref, kn_ref, vp_ref, vc_ref, vn_ref, o_ref, st_ref, *scratch,
                 sub_len, tqs, dil, rps):
    stage = scratch[0] if dil > 1 else None
    n = pl.program_id(1)
    n_sub = tqs // ATT_TQ
    heads_per_group = LANES // ATT_HEAD_DIM
    rows = heads_per_group * ATT_TQ
    qi = lax.broadcasted_iota(jnp.int32, (rows, ATT_TK), 0) % ATT_TQ
    kj = lax.broadcasted_iota(jnp.int32, (rows, ATT_TK), 1)
    band_bias = jnp.where((kj >= qi) & (kj <= qi + 2 * ATT_HALF), 0.0, NEG_FILL)
    key_col = lax.broadcasted_iota(jnp.int32, (1, ATT_TK), 1)
    lane = lax.broadcasted_iota(jnp.int32, (ATT_TQ, LANES), 1)
    first_head = lane < ATT_HEAD_DIM
    stat_lane = [(lane == _stat_lane(heads_per_group * g)) | (lane == _stat_lane(heads_per_group * g + 1))
                 for g in range(ATT_WIDTH // LANES)]
    ones = jnp.ones((ATT_TK, LANES), BF16)

    def window(prev_ref, cur_ref, next_ref, i, cols):
        lo, hi = i * ATT_TQ - ATT_HALF, (i + 1) * ATT_TQ + ATT_HALF
        parts = []
        if lo < 0:
            parts.append(prev_ref[:, cols])
        parts.append(cur_ref[max(lo, 0):min(hi, tqs), cols])
        if hi > tqs:
            parts.append(next_ref[:, cols])
        return parts[0] if len(parts) == 1 else jnp.concatenate(parts, axis=0)

    for j in range(rps):
        res = pl.program_id(2) * rps + j
        for i in range(n_sub):
            key0 = n * tqs + i * ATT_TQ - ATT_HALF
            bias = band_bias
            if i == 0:
                bias = bias + jnp.where(key_col >= -key0, 0.0, NEG_FILL)
            if i == n_sub - 1:
                bias = bias + jnp.where(key_col < sub_len - key0, 0.0, NEG_FILL)
            if dil == 1:
                rows_out = pl.ds(i * ATT_TQ, ATT_TQ)
            else:
                rows_out = pl.ds(i * ATT_TQ * dil + res, ATT_TQ, stride=dil)
            stats = jnp.zeros((ATT_TQ, LANES), F32)
            for g in range(ATT_WIDTH // LANES):
                cols = slice(j * ATT_WIDTH + g * LANES, j * ATT_WIDTH + (g + 1) * LANES)
                q2 = q_ref[i * ATT_TQ:(i + 1) * ATT_TQ, cols]
                zero = jnp.zeros_like(q2)
                qs = jnp.concatenate(
                    [jnp.where(first_head, q2, zero), jnp.where(first_head, zero, q2)], axis=0)
                kk = window(kp_ref, kc_ref, kn_ref, i, cols)
                s = lax.dot_general(qs, kk, (((1,), (1,)), ((), ())), preferred_element_type=F32) + bias
                m = jnp.max(s, axis=-1, keepdims=True)
                p = jnp.exp2((s - m).astype(BF16))
                vext = jnp.concatenate([window(vp_ref, vc_ref, vn_ref, i, cols), ones], axis=1)
                r = jnp.dot(p, vext, preferred_element_type=F32)
                o = jnp.where(first_head, r[:ATT_TQ, :LANES], r[ATT_TQ:, :LANES])
                l = jnp.where(first_head, r[:ATT_TQ, LANES:], r[ATT_TQ:, LANES:])
                if dil == 1:
                    o_ref[rows_out, g * LANES:(g + 1) * LANES] = (o / l).astype(o_ref.dtype)
                else:
                    stage[g, rows_out, :] = o / l
                lse2 = jnp.where(first_head, m[:ATT_TQ], m[ATT_TQ:]) + jnp.log2(l)
                stats = jnp.where(stat_lane[g], lse2, stats)
            st_ref[rows_out, :] = stats

    if dil > 1:
        @pl.when(pl.program_id(2) == pl.num_programs(2) - 1)
        def _():
            for g in range(ATT_WIDTH // LANES):
                o_ref[:, g * LANES:(g + 1) * LANES] = stage[g].astype(o_ref.dtype)


def _attn_branch(q, k, v, batch, seq_len, dil):
    sub_len = seq_len // dil
    t_pos = min(seq_len, ATT_POS_PER_STEP[dil > 1])
    tqs = t_pos // dil
    rps = max(1, min(dil, ATT_QUERIES_PER_STEP // tqs))
    assert seq_len % t_pos == 0 and tqs % ATT_TQ == 0 and dil % rps == 0
    halo_per_tile = tqs // ATT_HALF
    n_halo = sub_len // ATT_HALF
    view = lambda t: t.reshape(batch, sub_len, dil * ATT_WIDTH)
    cur = lambda b, n, r: (b, n, r)
    prev = lambda b, n, r: (b, jnp.maximum(n * halo_per_tile - 1, 0), r)
    nxt = lambda b, n, r: (b, jnp.minimum((n + 1) * halo_per_tile, n_halo - 1), r)
    main = pl.BlockSpec((None, tqs, rps * ATT_WIDTH), cur)
    halo_p = pl.BlockSpec((None, ATT_HALF, rps * ATT_WIDTH), prev)
    halo_n = pl.BlockSpec((None, ATT_HALF, rps * ATT_WIDTH), nxt)
    slabs = ATT_WIDTH // LANES
    return pl.pallas_call(
        functools.partial(_attn_kernel, sub_len=sub_len, tqs=tqs, dil=dil, rps=rps),
        grid=(batch, seq_len // t_pos, dil // rps),
        in_specs=[main, halo_p, main, halo_n, halo_p, main, halo_n],
        out_specs=[pl.BlockSpec((None, t_pos, ATT_WIDTH), lambda b, n, r: (b, n, 0)),
                   pl.BlockSpec((None, t_pos, LANES), lambda b, n, r: (b, n, 0))],
        out_shape=[jax.ShapeDtypeStruct((batch, seq_len, ATT_WIDTH), BF16),
                   jax.ShapeDtypeStruct((batch, seq_len, LANES), F32)],
        scratch_shapes=[pltpu.VMEM((slabs, t_pos, LANES), F32)] if dil > 1 else [],
        compiler_params=pltpu.CompilerParams(
            dimension_semantics=("parallel", "parallel", "arbitrary"), vmem_limit_bytes=VMEM_LIMIT),
    )(view(q), view(k), view(k), view(k), view(v), view(v), view(v))


def _hgrn_kernel(tri_f_ref, tri_b_ref,
                 qf_ref, gf_ref, kf_ref, vf_ref, qb_ref, gb_ref, kb_ref, vb_ref,
                 of_ref, ob_ref,
                 st_f, st_b, b_f, b_b, oi_f, oi_b, k32, *, th):
    n_sub = th // HG_SUBTILE
    n_chunks = HG_SUBTILE // HG_CHUNK
    local_rows = [slice(c * HG_CHUNK, (c + 1) * HG_CHUNK) for c in range(n_chunks)]

    @pl.when(pl.program_id(1) == 0)
    def _():
        st_f[...] = jnp.zeros_like(st_f)
        st_b[...] = jnp.zeros_like(st_b)

    def chunk_sums(tri_ref, g_ref, out):
        tri = tri_ref[...].astype(BF16)
        for s in range(n_sub):
            rows = slice(s * HG_SUBTILE, (s + 1) * HG_SUBTILE)
            g = g_ref[rows, :]
            hi = g.astype(BF16)
            lo = (g - hi.astype(F32)).astype(BF16)
            out[rows, :] = (jnp.dot(tri, hi, preferred_element_type=F32)
                            + jnp.dot(tri, lo, preferred_element_type=F32))

    chunk_sums(tri_f_ref, gf_ref, b_f)
    chunk_sums(tri_b_ref, gb_ref, b_b)

    ti = lax.broadcasted_iota(jnp.int32, (HG_CHUNK, HG_CHUNK), 0)
    si = lax.broadcasted_iota(jnp.int32, (HG_CHUNK, HG_CHUNK), 1)
    dirs = (
        (False, qf_ref, kf_ref, vf_ref, b_f, of_ref, st_f, oi_f, tri_f_ref, si <= ti,
         HG_CHUNK // 2 - 1, HG_CHUNK - 1),
        (True, qb_ref, kb_ref, vb_ref, b_b, ob_ref, st_b, oi_b, tri_b_ref, si >= ti,
         HG_CHUNK // 2, 0),
    )
    nt_dims = (((1,), (1,)), ((), ()))
    tn_dims = (((0,), (0,)), ((), ()))

    streams = [(d, h) for d in range(len(dirs)) for h in range(HG_HEADS)]
    for sub in range(n_sub):
        work = {}
        for d, h in streams:
            rev, q_ref, k_ref, v_ref, b, _, _, _, tri_ref, _, anchor, edge = dirs[d]
            row0 = (n_sub - 1 - sub if rev else sub) * HG_SUBTILE
            rows = slice(row0, row0 + HG_SUBTILE)
            cols = slice(h * HG_DK, (h + 1) * HG_DK)
            bt = b[rows, cols]
            mids = [bt[r.start + anchor:r.start + anchor + 1, :] for r in local_rows]
            edges = [bt[r.start + edge:r.start + edge + 1, :] for r in local_rows]
            edge_rows = jnp.concatenate([jnp.broadcast_to(e, (HG_CHUNK, HG_DK)) for e in edges], axis=0)
            qi32 = q_ref[rows, cols].astype(F32) * jnp.exp(bt)
            kd32 = k_ref[rows, cols].astype(F32) * jnp.exp(edge_rows - bt)
            qa = jnp.concatenate([qi32[r] * jnp.exp(-m) for r, m in zip(local_rows, mids)], axis=0)
            ka = jnp.concatenate(
                [kd32[r] * jnp.exp(m - e) for r, m, e in zip(local_rows, mids, edges)], axis=0)
            a = lax.dot_general(qa.astype(BF16), ka.astype(BF16), nt_dims, preferred_element_type=F32)
            pair_ok = tri_ref[...] > 0.5
            work[d, h] = dict(a=jnp.where(pair_ok, a, 0.0).astype(BF16), edges=edges, row0=row0,
                              qi=qi32.astype(BF16), kd=kd32.astype(BF16), v=v_ref[rows, cols], cols=cols)
        for key in streams:
            w = work[key]
            w["o_intra"] = jnp.dot(w["a"], w["v"], preferred_element_type=F32)
        for step in range(n_chunks):
            for d, h in streams:
                rev, _, _, _, _, o_ref, st, oi, _, _, _, _ = dirs[d]
                w = work[d, h]
                c = n_chunks - 1 - step if rev else step
                r = local_rows[c]
                out_rows = slice(w["row0"] + r.start, w["row0"] + r.stop)
                state = st[h]
                o_inter = lax.dot_general(w["qi"][r], state.astype(BF16), nt_dims,
                                          preferred_element_type=F32)
                oi[out_rows, w["cols"]] = o_inter
                o_ref[out_rows, w["cols"]] = (o_inter + w["o_intra"][r]).astype(o_ref.dtype)
                st[h] = state * jnp.exp(w["edges"][c]) + lax.dot_general(
                    w["v"][r], w["kd"][r], tn_dims, preferred_element_type=F32)

    safe = jnp.minimum(jnp.min(b_f[...]), jnp.min(b_b[...])) >= -SAFE_DECAY_LOG

    @pl.when(jnp.logical_not(safe))
    def _():
        for _, q_ref, k_ref, v_ref, b, o_ref, _, oi, _, mask, _, _ in dirs:
            k32[...] = k_ref[...].astype(F32)
            for h in range(HG_HEADS):
                cols = slice(h * HG_DK, (h + 1) * HG_DK)

                def per_chunk(c, carry):
                    r0 = pl.multiple_of(c * HG_CHUNK, HG_CHUNK)
                    bc = b[pl.ds(r0, HG_CHUNK), cols]
                    qc = q_ref[pl.ds(r0, HG_CHUNK), cols].astype(F32)

                    def per_key_group(s8, a):
                        k0 = pl.multiple_of(r0 + s8 * SUBLANES, SUBLANES)
                        b8 = b[pl.ds(k0, SUBLANES), cols]
                        k8 = k32[pl.ds(k0, SUBLANES), cols]
                        for j in range(SUBLANES):
                            w = qc * k8[j:j + 1, :] * jnp.exp(jnp.minimum(bc - b8[j:j + 1, :], 0.0))
                            a = jnp.where(si == s8 * SUBLANES + j, jnp.sum(w, axis=-1, keepdims=True), a)
                        return a

                    a = lax.fori_loop(0, HG_CHUNK // SUBLANES, per_key_group,
                                      jnp.zeros((HG_CHUNK, HG_CHUNK), F32))
                    a = jnp.where(mask, a, 0.0).astype(BF16)
                    o = oi[pl.ds(r0, HG_CHUNK), cols] + jnp.dot(
                        a, v_ref[pl.ds(r0, HG_CHUNK), cols], preferred_element_type=F32)
                    o_ref[pl.ds(r0, HG_CHUNK), cols] = o.astype(o_ref.dtype)
                    return carry

                lax.fori_loop(0, th // HG_CHUNK, per_chunk, 0)


def _block_triangular(th, upper):
    t = np.arange(th)
    same = (t[:, None] // HG_CHUNK) == (t[None, :] // HG_CHUNK)
    tri = (t[None, :] >= t[:, None]) if upper else (t[None, :] <= t[:, None])
    return jnp.asarray(same & tri, dtype=F32)


def _hgrn(qh, gf, kf, gb, kb, vh, batch, seq_len, th):
    nt = seq_len // th
    view = lambda t: t.reshape(batch, seq_len, HG_WIDTH)
    fwd = lambda b, i: (b, i, 0)
    bwd = lambda b, i: (b, nt - 1 - i, 0)
    const = lambda b, i: (0, 0)
    tile_f = pl.BlockSpec((None, th, HG_WIDTH), fwd)
    tile_b = pl.BlockSpec((None, th, HG_WIDTH), bwd)
    tri = pl.BlockSpec((HG_SUBTILE, HG_SUBTILE), const)
    tile_f32 = pltpu.VMEM((th, HG_WIDTH), F32)
    return pl.pallas_call(
        functools.partial(_hgrn_kernel, th=th),
        grid=(batch, nt),
        in_specs=[tri, tri, tile_f, tile_f, tile_f, tile_f, tile_b, tile_b, tile_b, tile_b],
        out_specs=[tile_f, tile_b],
        out_shape=[jax.ShapeDtypeStruct((batch, seq_len, HG_WIDTH), BF16)] * 2,
        scratch_shapes=[
            pltpu.VMEM((HG_HEADS, HG_DK, HG_DK), F32),
            pltpu.VMEM((HG_HEADS, HG_DK, HG_DK), F32),
            tile_f32, tile_f32,
            tile_f32, tile_f32,
            tile_f32,
        ],
        compiler_params=pltpu.CompilerParams(
            dimension_semantics=("parallel", "arbitrary"), vmem_limit_bytes=VMEM_LIMIT),
    )(_block_triangular(HG_SUBTILE, False), _block_triangular(HG_SUBTILE, True),
      view(qh), view(gf), view(kf), view(vh), view(qh), view(gb), view(kb), view(vh))


def _out_ffn_kernel(x_ref, o1_ref, o2_ref, o3_ref, s1_ref, s2_ref, s3_ref, hf_ref, hb_ref, gs_ref,
                    expand_ref, gh_ref, wout_ref, gpm_ref, gpf_ref, gpo_ref, wg_ref, wu_ref, wd_ref,
                    y_ref):
    def mixer_output(rows):
        lses = (s1_ref[rows, :], s2_ref[rows, :], s3_ref[rows, :])
        top = jnp.maximum(jnp.maximum(lses[0], lses[1]), lses[2])
        es = [jnp.exp2(s - top) for s in lses]
        den = es[0] + es[1] + es[2]
        att = None
        for e, o_ref in zip(es, (o1_ref, o2_ref, o3_ref)):
            w = jnp.dot((e / den).astype(BF16), expand_ref[...], preferred_element_type=F32)
            term = w * o_ref[rows, :].astype(F32)
            att = term if att is None else att + term
        o = hf_ref[rows, :].astype(F32) + hb_ref[rows, :].astype(F32)
        hg = jnp.concatenate(
            [_rms(o[:, h * HG_DK:(h + 1) * HG_DK]) * gh_ref[...] for h in range(HG_HEADS)], axis=1)
        hg = hg * gs_ref[rows, :].astype(F32)
        return jnp.concatenate([att, hg], axis=1).astype(BF16)

    tm = x_ref.shape[0]
    halves = [slice(i * (tm // 2), (i + 1) * (tm // 2)) for i in range(2)]
    dot = functools.partial(jnp.dot, preferred_element_type=F32)
    mix_in = [mixer_output(r) for r in halves]
    mix = [dot(m, wout_ref[...]) for m in mix_in]
    x1 = [x_ref[r, :] + _rms(m) * gpm_ref[...] for r, m in zip(halves, mix)]
    h2 = [(_rms(t) * gpf_ref[...]).astype(BF16) for t in x1]
    gate = [dot(t, wg_ref[...]) for t in h2]
    up = [dot(t, wu_ref[...]) for t in h2]
    act = [(_silu(g) * u).astype(BF16) for g, u in zip(gate, up)]
    ff = [dot(t, wd_ref[...]) for t in act]
    for r, t, f in zip(halves, x1, ff):
        y_ref[r, :] = t + _rms(f) * gpo_ref[...]


def _out_ffn(x, att_o, att_s, hf, hb, gs, g_hnorm, w_out, g_pm, g_pf, g_po, w_gate, w_up, w_down, tm):
    batch, seq_len, _ = x.shape
    col_head = np.arange(ATT_WIDTH) // ATT_HEAD_DIM
    expand = np.zeros((LANES, ATT_WIDTH), np.float32)
    expand[_stat_lane(col_head), np.arange(ATT_WIDTH)] = 1.0
    expand = jnp.asarray(expand, dtype=BF16)
    const = lambda b, i: (0, 0)
    tile = lambda w: pl.BlockSpec((None, tm, w), lambda b, i: (b, i, 0))
    whole = lambda a: pl.BlockSpec(a.shape, const, pipeline_mode=pl.Buffered(1))
    consts = (expand, g_hnorm, w_out, g_pm, g_pf, g_po, w_gate, w_up, w_down)
    return pl.pallas_call(
        _out_ffn_kernel,
        grid=(batch, seq_len // tm),
        in_specs=[tile(D_MODEL)] + [tile(ATT_WIDTH)] * 3 + [tile(LANES)] * 3 + [tile(HG_WIDTH)] * 3
                 + [whole(a) for a in consts],
        out_specs=tile(D_MODEL),
        out_shape=jax.ShapeDtypeStruct((batch, seq_len, D_MODEL), F32),
        compiler_params=pltpu.CompilerParams(
            dimension_semantics=("parallel", "parallel"), vmem_limit_bytes=VMEM_LIMIT),
    )(x, *att_o, *att_s, hf, hb, gs, *consts)


def _layer(x, tables, w_in, w_out, lb_fwd, lb_bwd, g_hnorm, g_pre_mix, g_post_mix, g_pre_ffn, g_post_ffn,
           w_gate, w_up, w_down, *, tm_in=IN_PROJ_ROWS, th=HG_TILE_ROWS, tm_out=OUT_FFN_ROWS):
    batch, seq_len, _ = x.shape
    x2 = x.reshape(batch * seq_len, D_MODEL)
    outs = _in_proj(x2, g_pre_mix, w_in, tables, lb_fwd, lb_bwd, seq_len, tm_in)
    n_att = 3 * len(DILATED_BRANCHES)
    qh, gf, kf, gb, kb, vh, gs = outs[n_att:]
    att = [_attn_branch(*outs[3 * b:3 * b + 3], batch, seq_len, dil)
           for b, (_, dil) in enumerate(DILATED_BRANCHES)]
    hf, hb = _hgrn(qh, gf, kf, gb, kb, vh, batch, seq_len, th)
    return _out_ffn(x, [o for o, _ in att], [s for _, s in att], hf, hb,
                    gs.reshape(batch, seq_len, HG_WIDTH), g_hnorm, w_out,
                    g_post_mix, g_pre_ffn, g_post_ffn, w_gate, w_up, w_down, tm_out)


def kernel(x_prompt, x_sample, w_in, w_out, lb_fwd, lb_bwd, g_hgrn_norm, g_pre_mix, g_post_mix,
           g_pre_ffn, g_post_ffn, w_gate, w_up, w_down):
    assert w_in.shape[0] == 1, "one layer"
    assert all(w // (2 * d) == ATT_HALF for w, d in DILATED_BRANCHES)
    params = (w_in[0].astype(BF16), w_out[0].astype(BF16), lb_fwd, lb_bwd, g_hgrn_norm,
              g_pre_mix, g_post_mix, g_pre_ffn, g_post_ffn,
              w_gate[0].astype(BF16), w_up[0].astype(BF16), w_down[0].astype(BF16))
    tables = _rope_tables(max(x_prompt.shape[1], x_sample.shape[1]), IN_PROJ_ROWS)
    return _layer(x_prompt, tables, *params), _layer(x_sample, tables, *params)
```

```python
import functools
import math

import jax
import jax.numpy as jnp
import numpy as np
from jax import lax
from jax.experimental import pallas as pl
from jax.experimental.pallas import tpu as pltpu

F32 = jnp.float32
BF16 = jnp.bfloat16

D_MODEL = 1024
ATT_HEADS = 8
ATT_HEAD_DIM = 64
ATT_WIDTH = ATT_HEADS * ATT_HEAD_DIM
DILATED_BRANCHES = ((128, 1), (512, 4), (2048, 16))
ROT_DIM = ATT_HEAD_DIM // 4
ROPE_THETA = 500000.0
HG_HEADS = 4
HG_DK = 128
HG_WIDTH = HG_HEADS * HG_DK
HG_CHUNK = 64
HG_SUBTILE = 256
GROUP_W = 512
N_GROUPS = 8
HG16_FIELDS = ("qh", "kf", "kb", "vh", "gs")
HG32_FIELDS = ("gf", "gb")
NORM_EPS = 1e-6
NEG_FILL = -1e30
LOG2E = math.log2(math.e)

LANES = 128
SUBLANES = 8
ATT_HALF = 64
ATT_TQ = 128
ATT_TK = ATT_TQ + 2 * ATT_HALF
ATT_QUERIES_PER_STEP = 4096
ATT_MAX_RESIDUES_PER_STEP = 8
ATT_POS_PER_STEP = (4096, 4096)
IN_PROJ_ROWS = 512
HG_TILE_ROWS = 1024
OUT_FFN_ROWS = 512
SAFE_DECAY_LOG = 80.0
VMEM_LIMIT = 56 * 1024 * 1024


def _sigmoid_pair(z):
    e = jnp.exp(-jnp.abs(z))
    big = 1.0 / (1.0 + e)
    small = e * big
    pos = z >= 0
    return jnp.where(pos, big, small), jnp.where(pos, small, big)


def _silu(z):
    s, _ = _sigmoid_pair(z)
    return z * s


def _rms(x):
    return x * lax.rsqrt(jnp.mean(x * x, axis=-1, keepdims=True) + NORM_EPS)


def _in_proj_kernel(x_ref, g_ref, w_ref, rope_base_ref, rope_delta_ref, lbf_ref, lbb_ref, *refs, tm):
    n_br = len(DILATED_BRANCHES)
    att_refs = refs[:n_br]
    hg16_ref, hg32_ref = refs[n_br:n_br + 2]
    u_scr = refs[n_br + 2]
    stages = refs[n_br + 3:]
    hg16 = lambda name: slice(HG16_FIELDS.index(name) * GROUP_W, (HG16_FIELDS.index(name) + 1) * GROUP_W)
    hg32 = lambda name: slice(HG32_FIELDS.index(name) * GROUP_W, (HG32_FIELDS.index(name) + 1) * GROUP_W)
    slabs = GROUP_W // LANES
    n_parts = 1
    part = tm // n_parts
    part_rows = [slice(p * part, (p + 1) * part) for p in range(n_parts)]
    h = [(_rms(x_ref[r, :]) * g_ref[...]).astype(BF16) for r in part_rows]

    def emit_dilated(t, which, p):
        assert DILATED_BRANCHES[0][1] == 1
        att_refs[0][part_rows[p], which * GROUP_W:(which + 1) * GROUP_W] = t.astype(BF16)
        for s in range(slabs):
            stages[0][s, part_rows[p], :] = t[:, s * LANES:(s + 1) * LANES]
        for b in range(1, n_br):
            dil, prev = DILATED_BRANCHES[b][1], DILATED_BRANCHES[b - 1][1]
            step = dil // prev
            out = att_refs[b]
            rows = slice(p * part // dil, (p + 1) * part // dil)
            for r in range(dil):
                r_prev, r_new = r % prev, r // prev
                for s in range(slabs):
                    piece = stages[b - 1][r_prev * slabs + s,
                                          pl.ds(r_new + step * rows.start, part // dil, stride=step), :]
                    if b + 1 < n_br:
                        stages[b][r * slabs + s, rows, :] = piece
                    lane0 = (which * dil + r) * GROUP_W + s * LANES
                    out[rows, lane0:lane0 + LANES] = piece.astype(BF16)

    def proj(j, p):
        return jnp.dot(h[p], w_ref[:, j * GROUP_W:(j + 1) * GROUP_W], preferred_element_type=F32)

    cb, sb = rope_base_ref[0:1, :], rope_base_ref[1:2, :]
    dt = rope_delta_ref
    reps = GROUP_W // LANES
    half = ROT_DIM // 2

    def rope(t, p):
        r = part_rows[p]
        cos = jnp.concatenate([cb * dt[0, r, :] - sb * dt[1, r, :] + dt[2, r, :]] * reps, axis=1)
        sina = jnp.concatenate([sb * dt[3, r, :] + cb * dt[4, r, :]] * reps, axis=1)
        sinb = jnp.concatenate([sb * dt[5, r, :] + cb * dt[6, r, :]] * reps, axis=1)
        return t * cos + pltpu.roll(t, GROUP_W - half, 1) * sina + pltpu.roll(t, half, 1) * sinb

    def lower_bound(lb_ref):
        a = lb_ref[...]
        e = jnp.exp(a - jnp.max(a, axis=0, keepdims=True))
        return e[0:1, :] / jnp.sum(e, axis=0, keepdims=True)

    def gates(z, lb, g_name, k_name, p):
        s_pos, s_neg = _sigmoid_pair(z)
        hg32_ref[part_rows[p], hg32(g_name)] = jnp.log(lb + (1.0 - lb) * s_pos)
        hg16_ref[part_rows[p], hg16(k_name)] = ((1.0 - lb) * s_neg).astype(BF16)

    def store_silu(name):
        def epilogue(t, p):
            hg16_ref[part_rows[p], hg16(name)] = _silu(t).astype(BF16)
        return epilogue

    def store_plain(t, p):
        hg16_ref[part_rows[p], hg16("vh")] = t.astype(BF16)

    epilogues = (
        lambda t, p: emit_dilated(rope(t, p) * (ATT_HEAD_DIM ** -0.5 * LOG2E), 0, p),
        lambda t, p: emit_dilated(rope(t, p), 1, p),
        lambda t, p: emit_dilated(t, 2, p),
        store_silu("qh"),
        lambda t, p: gates(t, lower_bound(lbf_ref), "gf", "kf", p),
        lambda t, p: gates(t, lower_bound(lbb_ref), "gb", "kb", p),
        store_plain,
        store_silu("gs"),
    )

    base = jnp.minimum(pl.program_id(0), 0)
    order = (0, 1, 2, 3, 4, 5, 7, 6)
    for p in range(n_parts):
        u_scr[base, part_rows[p], :] = proj(order[0], p)
    for i, j in enumerate(order):
        for p in range(n_parts):
            if i + 1 < N_GROUPS:
                u_scr[base + (i + 1) % 2, part_rows[p], :] = proj(order[i + 1], p)
            epilogues[j](u_scr[base + i % 2, part_rows[p], :], p)


def _in_proj(x2, g_pre, w_in, tables, lb_f, lb_b, seq_len, tm):
    n = x2.shape[0]
    n_pos_tiles = seq_len // tm
    row = lambda i: (i, 0)
    const = lambda i: (0, 0)
    rope_base, rope_delta = tables
    specs = [pl.BlockSpec((tm // dil, 3 * dil * GROUP_W), row) for _, dil in DILATED_BRANCHES]
    shapes = [jax.ShapeDtypeStruct((n // dil, 3 * dil * GROUP_W), BF16) for _, dil in DILATED_BRANCHES]
    for fields, dtype in ((HG16_FIELDS, BF16), (HG32_FIELDS, F32)):
        specs.append(pl.BlockSpec((tm, len(fields) * GROUP_W), row))
        shapes.append(jax.ShapeDtypeStruct((n, len(fields) * GROUP_W), dtype))
    return pl.pallas_call(
        functools.partial(_in_proj_kernel, tm=tm),
        grid=(n // tm,),
        in_specs=[
            pl.BlockSpec((tm, D_MODEL), row),
            pl.BlockSpec((1, D_MODEL), const),
            pl.BlockSpec((D_MODEL, N_GROUPS * GROUP_W), const, pipeline_mode=pl.Buffered(1)),
            pl.BlockSpec((None, SUBLANES, LANES), lambda i: (i % n_pos_tiles, 0, 0)),
            pl.BlockSpec(rope_delta.shape, lambda i: (0, 0, 0), pipeline_mode=pl.Buffered(1)),
            pl.BlockSpec(lb_f.shape, const),
            pl.BlockSpec(lb_b.shape, const),
        ],
        out_specs=specs,
        out_shape=shapes,
        scratch_shapes=[pltpu.VMEM((2, tm, GROUP_W), F32)]
                       + [pltpu.VMEM((dil * GROUP_W // LANES, tm // dil, LANES), F32)
                          for _, dil in DILATED_BRANCHES[:-1]],
        compiler_params=pltpu.CompilerParams(
            dimension_semantics=("parallel",), vmem_limit_bytes=VMEM_LIMIT),
    )(x2, g_pre, w_in, rope_base, rope_delta, lb_f, lb_b)


def _rope_tables(seq_len, tm):
    half = ROT_DIM // 2
    dim = np.arange(LANES) % ATT_HEAD_DIM
    first, second = dim < half, (dim >= half) & (dim < ROT_DIM)
    rotary = first | second
    inv_freq = ROPE_THETA ** (-jnp.arange(0, ROT_DIM, 2, dtype=F32) / ROT_DIM)
    freq = inv_freq[dim % half][None, :]
    start = jnp.arange(0, seq_len, tm, dtype=F32)[:, None] * freq
    offset = jnp.arange(tm, dtype=F32)[:, None] * freq
    pad = jnp.zeros((seq_len // tm, SUBLANES - 2, LANES), F32)
    base = jnp.concatenate([jnp.cos(start)[:, None], jnp.sin(start)[:, None], pad], axis=1)
    cd, sd = jnp.cos(offset), jnp.sin(offset)
    zero = jnp.zeros_like(cd)
    delta = jnp.stack([
        jnp.where(rotary, cd, zero), jnp.where(rotary, sd, zero), jnp.where(rotary, zero, 1.0),
        jnp.where(first, -cd, zero), jnp.where(first, -sd, zero),
        jnp.where(second, cd, zero), jnp.where(second, sd, zero)])
    return base, delta


def _stat_lane(head):
    return (head % 2) * ATT_HEAD_DIM + head


def _attn_kernel(q_ref, kp_ref, kc_ref, kn_ref, vp_ref, vc_ref, vn_ref, o_ref, st_ref, *scratch,
                 sub_len, tqs, dil, rps):
    stage = scratch[0] if dil > 1 else None
    n = pl.program_id(1)
    n_sub = tqs // ATT_TQ
    heads_per_group = LANES // ATT_HEAD_DIM
    rows = heads_per_group * ATT_TQ
    qi = lax.broadcasted_iota(jnp.int32, (rows, ATT_TK), 0) % ATT_TQ
    kj = lax.broadcasted_iota(jnp.int32, (rows, ATT_TK), 1)
    band_bias = jnp.where((kj >= qi) & (kj <= qi + 2 * ATT_HALF), 0.0, NEG_FILL)
    key_col = lax.broadcasted_iota(jnp.int32, (1, ATT_TK), 1)
    lane = lax.broadcasted_iota(jnp.int32, (ATT_TQ, LANES), 1)
    first_head = lane < ATT_HEAD_DIM
    stat_lane = [(lane == _stat_lane(heads_per_group * g)) | (lane == _stat_lane(heads_per_group * g + 1))
                 for g in range(ATT_WIDTH // LANES)]
    ones = jnp.ones((ATT_TK, LANES), BF16)

    def window(prev_ref, cur_ref, next_ref, i, cols):
        lo, hi = i * ATT_TQ - ATT_HALF, (i + 1) * ATT_TQ + ATT_HALF
        parts = []
        if lo < 0:
            parts.append(prev_ref[:, cols])
        parts.append(cur_ref[max(lo, 0):min(hi, tqs), cols])
        if hi > tqs:
            parts.append(next_ref[:, cols])
        return parts[0] if len(parts) == 1 else jnp.concatenate(parts, axis=0)

    for j in range(rps):
        res = pl.program_id(2) * rps + j
        for i in range(n_sub):
            key0 = n * tqs + i * ATT_TQ - ATT_HALF
            bias = band_bias
            if i == 0:
                bias = bias + jnp.where(key_col >= -key0, 0.0, NEG_FILL)
            if i == n_sub - 1:
                bias = bias + jnp.where(key_col < sub_len - key0, 0.0, NEG_FILL)
            if dil == 1:
                rows_out = pl.ds(i * ATT_TQ, ATT_TQ)
            else:
                rows_out = pl.ds(i * ATT_TQ * dil + res, ATT_TQ, stride=dil)
            stats = jnp.zeros((ATT_TQ, LANES), F32)
            for g in range(ATT_WIDTH // LANES):
                cols = slice(j * ATT_WIDTH + g * LANES, j * ATT_WIDTH + (g + 1) * LANES)
                q2 = q_ref[i * ATT_TQ:(i + 1) * ATT_TQ, cols]
                zero = jnp.zeros_like(q2)
                qs = jnp.concatenate(
                    [jnp.where(first_head, q2, zero), jnp.where(first_head, zero, q2)], axis=0)
                kk = window(kp_ref, kc_ref, kn_ref, i, cols)
                s = lax.dot_general(qs, kk, (((1,), (1,)), ((), ())), preferred_element_type=F32) + bias
                m = jnp.max(s, axis=-1, keepdims=True)
                p = jnp.exp2((s - m).astype(BF16))
                vext = jnp.concatenate([window(vp_ref, vc_ref, vn_ref, i, cols), ones], axis=1)
                r = jnp.dot(p, vext, preferred_element_type=F32)
                o = jnp.where(first_head, r[:ATT_TQ, :LANES], r[ATT_TQ:, :LANES])
                l = jnp.where(first_head, r[:ATT_TQ, LANES:], r[ATT_TQ:, LANES:])
                if dil == 1:
                    o_ref[rows_out, g * LANES:(g + 1) * LANES] = (o / l).astype(o_ref.dtype)
                else:
                    stage[g, rows_out, :] = o / l
                lse2 = jnp.where(first_head, m[:ATT_TQ], m[ATT_TQ:]) + jnp.log2(l)
                stats = jnp.where(stat_lane[g], lse2, stats)
            st_ref[rows_out, :] = stats

    if dil > 1:
        @pl.when(pl.program_id(2) == pl.num_programs(2) - 1)
        def _():
            for g in range(ATT_WIDTH // LANES):
                o_ref[:, g * LANES:(g + 1) * LANES] = stage[g].astype(o_ref.dtype)


def _attn_branch(qkv, batch, seq_len, dil):
    sub_len = seq_len // dil
    t_pos = min(seq_len, ATT_POS_PER_STEP[dil > 1])
    tqs = t_pos // dil
    rps = max(1, min(dil, ATT_MAX_RESIDUES_PER_STEP, ATT_QUERIES_PER_STEP // tqs))
    assert seq_len % t_pos == 0 and tqs % ATT_TQ == 0 and dil % rps == 0
    halo_per_tile = tqs // ATT_HALF
    n_halo = sub_len // ATT_HALF
    n_blocks = dil // rps
    qkv = qkv.reshape(batch, sub_len, 3 * dil * ATT_WIDTH)

    def specs(which):
        cur = lambda b, n, r: (b, n, which * n_blocks + r)
        prev = lambda b, n, r: (b, jnp.maximum(n * halo_per_tile - 1, 0), which * n_blocks + r)
        nxt = lambda b, n, r: (b, jnp.minimum((n + 1) * halo_per_tile, n_halo - 1), which * n_blocks + r)
        return (pl.BlockSpec((None, ATT_HALF, rps * ATT_WIDTH), prev),
                pl.BlockSpec((None, tqs, rps * ATT_WIDTH), cur),
                pl.BlockSpec((None, ATT_HALF, rps * ATT_WIDTH), nxt))

    slabs = ATT_WIDTH // LANES
    return pl.pallas_call(
        functools.partial(_attn_kernel, sub_len=sub_len, tqs=tqs, dil=dil, rps=rps),
        grid=(batch, seq_len // t_pos, dil // rps),
        in_specs=[specs(0)[1], *specs(1), *specs(2)],
        out_specs=[pl.BlockSpec((None, t_pos, ATT_WIDTH), lambda b, n, r: (b, n, 0)),
                   pl.BlockSpec((None, t_pos, LANES), lambda b, n, r: (b, n, 0))],
        out_shape=[jax.ShapeDtypeStruct((batch, seq_len, ATT_WIDTH), BF16),
                   jax.ShapeDtypeStruct((batch, seq_len, LANES), F32)],
        scratch_shapes=[pltpu.VMEM((slabs, t_pos, LANES), F32)] if dil > 1 else [],
        compiler_params=pltpu.CompilerParams(
            dimension_semantics=("parallel", "parallel", "arbitrary"), vmem_limit_bytes=VMEM_LIMIT),
    )(*([qkv] * 7))


def _hgrn_kernel(tri_f_ref, tri_b_ref,
                 qf_ref, gf_ref, kf_ref, vf_ref, qb_ref, gb_ref, kb_ref, vb_ref,
                 of_ref, ob_ref,
                 st_f, st_b, b_f, b_b, oi_f, oi_b, k32, *, th):
    n_sub = th // HG_SUBTILE
    n_chunks = HG_SUBTILE // HG_CHUNK
    local_rows = [slice(c * HG_CHUNK, (c + 1) * HG_CHUNK) for c in range(n_chunks)]

    @pl.when(pl.program_id(1) == 0)
    def _():
        st_f[...] = jnp.zeros_like(st_f)
        st_b[...] = jnp.zeros_like(st_b)

    def chunk_sums(tri_ref, g_ref, out):
        tri = tri_ref[...].astype(BF16)
        for s in range(n_sub):
            rows = slice(s * HG_SUBTILE, (s + 1) * HG_SUBTILE)
            g = g_ref[rows, :]
            hi = g.astype(BF16)
            lo = (g - hi.astype(F32)).astype(BF16)
            out[rows, :] = (jnp.dot(tri, hi, preferred_element_type=F32)
                            + jnp.dot(tri, lo, preferred_element_type=F32))

    chunk_sums(tri_f_ref, gf_ref, b_f)
    chunk_sums(tri_b_ref, gb_ref, b_b)

    ti = lax.broadcasted_iota(jnp.int32, (HG_CHUNK, HG_CHUNK), 0)
    si = lax.broadcasted_iota(jnp.int32, (HG_CHUNK, HG_CHUNK), 1)
    dirs = (
        (False, qf_ref, kf_ref, vf_ref, b_f, of_ref, st_f, oi_f, tri_f_ref, si <= ti,
         HG_CHUNK // 2 - 1, HG_CHUNK - 1),
        (True, qb_ref, kb_ref, vb_ref, b_b, ob_ref, st_b, oi_b, tri_b_ref, si >= ti,
         HG_CHUNK // 2, 0),
    )
    nt_dims = (((1,), (1,)), ((), ()))
    tn_dims = (((0,), (0,)), ((), ()))

    streams = [(d, h) for d in range(len(dirs)) for h in range(HG_HEADS)]
    for sub in range(n_sub):
        work = {}
        for d, h in streams:
            rev, q_ref, k_ref, v_ref, b, _, _, _, tri_ref, _, anchor, edge = dirs[d]
            row0 = (n_sub - 1 - sub if rev else sub) * HG_SUBTILE
            rows = slice(row0, row0 + HG_SUBTILE)
            cols = slice(h * HG_DK, (h + 1) * HG_DK)
            bt = b[rows, cols]
            mids = [bt[r.start + anchor:r.start + anchor + 1, :] for r in local_rows]
            edges = [bt[r.start + edge:r.start + edge + 1, :] for r in local_rows]
            edge_rows = jnp.concatenate([jnp.broadcast_to(e, (HG_CHUNK, HG_DK)) for e in edges], axis=0)
            qi32 = q_ref[rows, cols].astype(F32) * jnp.exp(bt)
            kd32 = k_ref[rows, cols].astype(F32) * jnp.exp(edge_rows - bt)
            qa = jnp.concatenate([qi32[r] * jnp.exp(-m) for r, m in zip(local_rows, mids)], axis=0)
            ka = jnp.concatenate(
                [kd32[r] * jnp.exp(m - e) for r, m, e in zip(local_rows, mids, edges)], axis=0)
            a = lax.dot_general(qa.astype(BF16), ka.astype(BF16), nt_dims, preferred_element_type=F32)
            pair_ok = tri_ref[...] > 0.5
            work[d, h] = dict(a=jnp.where(pair_ok, a, 0.0).astype(BF16), edges=edges, row0=row0,
                              qi=qi32.astype(BF16), kd=kd32.astype(BF16), v=v_ref[rows, cols], cols=cols)
        for key in streams:
            w = work[key]
            w["o_intra"] = jnp.dot(w["a"], w["v"], preferred_element_type=F32)
        for step in range(n_chunks):
            for d, h in streams:
                rev, _, _, _, _, o_ref, st, oi, _, _, _, _ = dirs[d]
                w = work[d, h]
                c = n_chunks - 1 - step if rev else step
                r = local_rows[c]
                out_rows = slice(w["row0"] + r.start, w["row0"] + r.stop)
                state = st[h]
                o_inter = lax.dot_general(w["qi"][r], state.astype(BF16), nt_dims,
                                          preferred_element_type=F32)
                oi[out_rows, w["cols"]] = o_inter
                o_ref[out_rows, w["cols"]] = (o_inter + w["o_intra"][r]).astype(o_ref.dtype)
                st[h] = state * jnp.exp(w["edges"][c]) + lax.dot_general(
                    w["v"][r], w["kd"][r], tn_dims, preferred_element_type=F32)

    safe = jnp.minimum(jnp.min(b_f[...]), jnp.min(b_b[...])) >= -SAFE_DECAY_LOG

    @pl.when(jnp.logical_not(safe))
    def _():
        for _, q_ref, k_ref, v_ref, b, o_ref, _, oi, _, mask, _, _ in dirs:
            k32[...] = k_ref[...].astype(F32)
            for h in range(HG_HEADS):
                cols = slice(h * HG_DK, (h + 1) * HG_DK)

                def per_chunk(c, carry):
                    r0 = pl.multiple_of(c * HG_CHUNK, HG_CHUNK)
                    bc = b[pl.ds(r0, HG_CHUNK), cols]
                    qc = q_ref[pl.ds(r0, HG_CHUNK), cols].astype(F32)

                    def per_key_group(s8, a):
                        k0 = pl.multiple_of(r0 + s8 * SUBLANES, SUBLANES)
                        b8 = b[pl.ds(k0, SUBLANES), cols]
                        k8 = k32[pl.ds(k0, SUBLANES), cols]
                        for j in range(SUBLANES):
                            w = qc * k8[j:j + 1, :] * jnp.exp(jnp.minimum(bc - b8[j:j + 1, :], 0.0))
                            a = jnp.where(si == s8 * SUBLANES + j, jnp.sum(w, axis=-1, keepdims=True), a)
                        return a

                    a = lax.fori_loop(0, HG_CHUNK // SUBLANES, per_key_group,
                                      jnp.zeros((HG_CHUNK, HG_CHUNK), F32))
                    a = jnp.where(mask, a, 0.0).astype(BF16)
                    o = oi[pl.ds(r0, HG_CHUNK), cols] + jnp.dot(
                        a, v_ref[pl.ds(r0, HG_CHUNK), cols], preferred_element_type=F32)
                    o_ref[pl.ds(r0, HG_CHUNK), cols] = o.astype(o_ref.dtype)
                    return carry

                lax.fori_loop(0, th // HG_CHUNK, per_chunk, 0)


def _block_triangular(th, upper):
    t = np.arange(th)
    same = (t[:, None] // HG_CHUNK) == (t[None, :] // HG_CHUNK)
    tri = (t[None, :] >= t[:, None]) if upper else (t[None, :] <= t[:, None])
    return jnp.asarray(same & tri, dtype=F32)


def _hgrn(hg16, hg32, batch, seq_len, th):
    nt = seq_len // th
    hg16 = hg16.reshape(batch, seq_len, len(HG16_FIELDS) * GROUP_W)
    hg32 = hg32.reshape(batch, seq_len, len(HG32_FIELDS) * GROUP_W)
    const = lambda b, i: (0, 0)

    def tile(fields, name, rev):
        j = fields.index(name)
        index = (lambda b, i: (b, nt - 1 - i, j)) if rev else (lambda b, i: (b, i, j))
        return pl.BlockSpec((None, th, HG_WIDTH), index)

    tile_f = pl.BlockSpec((None, th, HG_WIDTH), lambda b, i: (b, i, 0))
    tile_b = pl.BlockSpec((None, th, HG_WIDTH), lambda b, i: (b, nt - 1 - i, 0))
    tri = pl.BlockSpec((HG_SUBTILE, HG_SUBTILE), const)
    tile_f32 = pltpu.VMEM((th, HG_WIDTH), F32)
    return pl.pallas_call(
        functools.partial(_hgrn_kernel, th=th),
        grid=(batch, nt),
        in_specs=[tri, tri,
                  tile(HG16_FIELDS, "qh", False), tile(HG32_FIELDS, "gf", False),
                  tile(HG16_FIELDS, "kf", False), tile(HG16_FIELDS, "vh", False),
                  tile(HG16_FIELDS, "qh", True), tile(HG32_FIELDS, "gb", True),
                  tile(HG16_FIELDS, "kb", True), tile(HG16_FIELDS, "vh", True)],
        out_specs=[tile_f, tile_b],
        out_shape=[jax.ShapeDtypeStruct((batch, seq_len, HG_WIDTH), BF16)] * 2,
        scratch_shapes=[
            pltpu.VMEM((HG_HEADS, HG_DK, HG_DK), F32),
            pltpu.VMEM((HG_HEADS, HG_DK, HG_DK), F32),
            tile_f32, tile_f32,
            tile_f32, tile_f32,
            tile_f32,
        ],
        compiler_params=pltpu.CompilerParams(
            dimension_semantics=("parallel", "arbitrary"), vmem_limit_bytes=VMEM_LIMIT),
    )(_block_triangular(HG_SUBTILE, False), _block_triangular(HG_SUBTILE, True),
      hg16, hg32, hg16, hg16, hg16, hg32, hg16, hg16)


def _out_ffn_kernel(x_ref, o1_ref, o2_ref, o3_ref, s1_ref, s2_ref, s3_ref, hf_ref, hb_ref, gs_ref,
                    expand_ref, gh_ref, wout_ref, gpm_ref, gpf_ref, gpo_ref, wg_ref, wu_ref, wd_ref,
                    y_ref):
    def mixer_output(rows):
        lses = (s1_ref[rows, :], s2_ref[rows, :], s3_ref[rows, :])
        top = jnp.maximum(jnp.maximum(lses[0], lses[1]), lses[2])
        es = [jnp.exp2(s - top) for s in lses]
        den = es[0] + es[1] + es[2]
        att = None
        for e, o_ref in zip(es, (o1_ref, o2_ref, o3_ref)):
            w = jnp.dot((e / den).astype(BF16), expand_ref[...], preferred_element_type=F32)
            term = w * o_ref[rows, :].astype(F32)
            att = term if att is None else att + term
        o = hf_ref[rows, :].astype(F32) + hb_ref[rows, :].astype(F32)
        hg = jnp.concatenate(
            [_rms(o[:, h * HG_DK:(h + 1) * HG_DK]) * gh_ref[...] for h in range(HG_HEADS)], axis=1)
        hg = hg * gs_ref[rows, :].astype(F32)
        return jnp.concatenate([att, hg], axis=1).astype(BF16)

    tm = x_ref.shape[0]
    halves = [slice(i * (tm // 2), (i + 1) * (tm // 2)) for i in range(2)]
    dot = functools.partial(jnp.dot, preferred_element_type=F32)
    mix_in = [mixer_output(r) for r in halves]
    mix = [dot(m, wout_ref[...]) for m in mix_in]
    x1 = [x_ref[r, :] + _rms(m) * gpm_ref[...] for r, m in zip(halves, mix)]
    h2 = [(_rms(t) * gpf_ref[...]).astype(BF16) for t in x1]
    gate = [dot(t, wg_ref[...]) for t in h2]
    up = [dot(t, wu_ref[...]) for t in h2]
    act = [(_silu(g) * u).astype(BF16) for g, u in zip(gate, up)]
    ff = [dot(t, wd_ref[...]) for t in act]
    for r, t, f in zip(halves, x1, ff):
        y_ref[r, :] = t + _rms(f) * gpo_ref[...]


def _out_ffn(x, att_o, att_s, hf, hb, gs, g_hnorm, w_out, g_pm, g_pf, g_po, w_gate, w_up, w_down, tm):
    batch, seq_len, _ = x.shape
    col_head = np.arange(ATT_WIDTH) // ATT_HEAD_DIM
    expand = np.zeros((LANES, ATT_WIDTH), np.float32)
    expand[_stat_lane(col_head), np.arange(ATT_WIDTH)] = 1.0
    expand = jnp.asarray(expand, dtype=BF16)
    const = lambda b, i: (0, 0)
    tile = lambda w: pl.BlockSpec((None, tm, w), lambda b, i: (b, i, 0))
    whole = lambda a: pl.BlockSpec(a.shape, const, pipeline_mode=pl.Buffered(1))
    consts = (expand, g_hnorm, w_out, g_pm, g_pf, g_po, w_gate, w_up, w_down)
    return pl.pallas_call(
        _out_ffn_kernel,
        grid=(batch, seq_len // tm),
        in_specs=[tile(D_MODEL)] + [tile(ATT_WIDTH)] * 3 + [tile(LANES)] * 3 + [tile(HG_WIDTH)] * 2
                 + [pl.BlockSpec((None, tm, GROUP_W), lambda b, i: (b, i, HG16_FIELDS.index("gs")))]
                 + [whole(a) for a in consts],
        out_specs=tile(D_MODEL),
        out_shape=jax.ShapeDtypeStruct((batch, seq_len, D_MODEL), F32),
        compiler_params=pltpu.CompilerParams(
            dimension_semantics=("parallel", "parallel"), vmem_limit_bytes=VMEM_LIMIT),
    )(x, *att_o, *att_s, hf, hb, gs, *consts)


def _layer(x, tables, w_in, w_out, lb_fwd, lb_bwd, g_hnorm, g_pre_mix, g_post_mix, g_pre_ffn, g_post_ffn,
           w_gate, w_up, w_down, *, tm_in=IN_PROJ_ROWS, th=HG_TILE_ROWS, tm_out=OUT_FFN_ROWS):
    batch, seq_len, _ = x.shape
    x2 = x.reshape(batch * seq_len, D_MODEL)
    *qkv, hg16, hg32 = _in_proj(x2, g_pre_mix, w_in, tables, lb_fwd, lb_bwd, seq_len, tm_in)
    att = [_attn_branch(t, batch, seq_len, dil) for t, (_, dil) in zip(qkv, DILATED_BRANCHES)]
    hf, hb = _hgrn(hg16, hg32, batch, seq_len, th)
    return _out_ffn(x, [o for o, _ in att], [s for _, s in att], hf, hb,
                    hg16.reshape(batch, seq_len, len(HG16_FIELDS) * GROUP_W), g_hnorm, w_out,
                    g_post_mix, g_pre_ffn, g_post_ffn, w_gate, w_up, w_down, tm_out)


def kernel(x_prompt, x_sample, w_in, w_out, lb_fwd, lb_bwd, g_hgrn_norm, g_pre_mix, g_post_mix,
           g_pre_ffn, g_post_ffn, w_gate, w_up, w_down):
    assert w_in.shape[0] == 1, "one layer"
    assert all(w // (2 * d) == ATT_HALF for w, d in DILATED_BRANCHES)
    params = (w_in[0].astype(BF16), w_out[0].astype(BF16), lb_fwd, lb_bwd, g_hgrn_norm,
              g_pre_mix, g_post_mix, g_pre_ffn, g_post_ffn,
              w_gate[0].astype(BF16), w_up[0].astype(BF16), w_down[0].astype(BF16))
    tables = _rope_tables(max(x_prompt.shape[1], x_sample.shape[1]), IN_PROJ_ROWS)
    return _layer(x_prompt, tables, *params), _layer(x_sample, tables, *params)
```

```python
import functools
import math

import jax
import jax.numpy as jnp
import numpy as np
from jax import lax
from jax.experimental import pallas as pl
from jax.experimental.pallas import tpu as pltpu

F32 = jnp.float32
BF16 = jnp.bfloat16

D_MODEL = 1024
ATT_HEADS = 8
ATT_HEAD_DIM = 64
ATT_WIDTH = ATT_HEADS * ATT_HEAD_DIM
DILATED_BRANCHES = ((128, 1), (512, 4), (2048, 16))
ROT_DIM = ATT_HEAD_DIM // 4
ROPE_THETA = 500000.0
HG_HEADS = 4
HG_DK = 128
HG_WIDTH = HG_HEADS * HG_DK
HG_CHUNK = 64
HG_SUBTILE = 256
GROUP_W = 512
N_GROUPS = 8
HG16_FIELDS = ("qh", "kf", "kb", "vh", "gs")
HG32_FIELDS = ("gf", "gb")
NORM_EPS = 1e-6
NEG_FILL = -1e30
LOG2E = math.log2(math.e)

LANES = 128
SUBLANES = 8
ATT_HALF = 64
ATT_TQ = 128
ATT_TK = ATT_TQ + 2 * ATT_HALF
ATT_QUERIES_PER_STEP = 2048
ATT_POS_PER_STEP = (4096, 4096)
IN_PROJ_ROWS = 512
HG_TILE_ROWS = 1024
OUT_FFN_ROWS = 512
SAFE_DECAY_LOG = 80.0
SAFE_Q_PEAK = 1e3
VMEM_LIMIT = 56 * 1024 * 1024


def _sigmoid_pair(z):
    e = jnp.exp(-jnp.abs(z))
    big = 1.0 / (1.0 + e)
    small = e * big
    pos = z >= 0
    return jnp.where(pos, big, small), jnp.where(pos, small, big)


def _silu(z):
    s, _ = _sigmoid_pair(z)
    return z * s


def _rms(x):
    return x * lax.rsqrt(jnp.mean(x * x, axis=-1, keepdims=True) + NORM_EPS)


def _in_proj_kernel(x_ref, g_ref, w_ref, rope_base_ref, rope_delta_ref, lbf_ref, lbb_ref, *refs, tm):
    n_br = len(DILATED_BRANCHES)
    att_refs = refs[:n_br]
    hg16_ref, hg32_ref = refs[n_br:n_br + 2]
    u_scr = refs[n_br + 2]
    stages = refs[n_br + 3:]
    hg16 = lambda name: slice(HG16_FIELDS.index(name) * GROUP_W, (HG16_FIELDS.index(name) + 1) * GROUP_W)
    hg32 = lambda name: slice(HG32_FIELDS.index(name) * GROUP_W, (HG32_FIELDS.index(name) + 1) * GROUP_W)
    slabs = GROUP_W // LANES
    n_parts = 1
    part = tm // n_parts
    part_rows = [slice(p * part, (p + 1) * part) for p in range(n_parts)]
    h = [(_rms(x_ref[r, :]) * g_ref[...]).astype(BF16) for r in part_rows]

    def emit_dilated(t, which, p):
        assert DILATED_BRANCHES[0][1] == 1
        att_refs[0][part_rows[p], which * GROUP_W:(which + 1) * GROUP_W] = t.astype(BF16)
        for s in range(slabs):
            stages[0][s, part_rows[p], :] = t[:, s * LANES:(s + 1) * LANES]
        for b in range(1, n_br):
            dil, prev = DILATED_BRANCHES[b][1], DILATED_BRANCHES[b - 1][1]
            step = dil // prev
            out = att_refs[b]
            rows = slice(p * part // dil, (p + 1) * part // dil)
            for r in range(dil):
                r_prev, r_new = r % prev, r // prev
                for s in range(slabs):
                    piece = stages[b - 1][r_prev * slabs + s,
                                          pl.ds(r_new + step * rows.start, part // dil, stride=step), :]
                    if b + 1 < n_br:
                        stages[b][r * slabs + s, rows, :] = piece
                    lane0 = (which * dil + r) * GROUP_W + s * LANES
                    out[rows, lane0:lane0 + LANES] = piece.astype(BF16)

    def proj(j, p):
        return jnp.dot(h[p], w_ref[:, j * GROUP_W:(j + 1) * GROUP_W], preferred_element_type=F32)

    cb, sb = rope_base_ref[0:1, :], rope_base_ref[1:2, :]
    dt = rope_delta_ref
    reps = GROUP_W // LANES
    half = ROT_DIM // 2

    def rope(t, p):
        r = part_rows[p]
        cos = jnp.concatenate([cb * dt[0, r, :] - sb * dt[1, r, :] + dt[2, r, :]] * reps, axis=1)
        sina = jnp.concatenate([sb * dt[3, r, :] + cb * dt[4, r, :]] * reps, axis=1)
        sinb = jnp.concatenate([sb * dt[5, r, :] + cb * dt[6, r, :]] * reps, axis=1)
        return t * cos + pltpu.roll(t, GROUP_W - half, 1) * sina + pltpu.roll(t, half, 1) * sinb

    def lower_bound(lb_ref):
        a = lb_ref[...]
        e = jnp.exp(a - jnp.max(a, axis=0, keepdims=True))
        return e[0:1, :] / jnp.sum(e, axis=0, keepdims=True)

    def gates(z, lb, g_name, k_name, p):
        s_pos, s_neg = _sigmoid_pair(z)
        hg32_ref[part_rows[p], hg32(g_name)] = jnp.log(lb + (1.0 - lb) * s_pos)
        hg16_ref[part_rows[p], hg16(k_name)] = ((1.0 - lb) * s_neg).astype(BF16)

    def store_silu(name):
        def epilogue(t, p):
            hg16_ref[part_rows[p], hg16(name)] = _silu(t).astype(BF16)
        return epilogue

    def store_plain(t, p):
        hg16_ref[part_rows[p], hg16("vh")] = t.astype(BF16)

    epilogues = (
        lambda t, p: emit_dilated(rope(t, p) * (ATT_HEAD_DIM ** -0.5 * LOG2E), 0, p),
        lambda t, p: emit_dilated(rope(t, p), 1, p),
        lambda t, p: emit_dilated(t, 2, p),
        store_silu("qh"),
        lambda t, p: gates(t, lower_bound(lbf_ref), "gf", "kf", p),
        lambda t, p: gates(t, lower_bound(lbb_ref), "gb", "kb", p),
        store_plain,
        store_silu("gs"),
    )

    base = jnp.minimum(pl.program_id(0), 0)
    order = (0, 1, 2, 3, 4, 5, 7, 6)
    for p in range(n_parts):
        u_scr[base, part_rows[p], :] = proj(order[0], p)
    for i, j in enumerate(order):
        for p in range(n_parts):
            if i + 1 < N_GROUPS:
                u_scr[base + (i + 1) % 2, part_rows[p], :] = proj(order[i + 1], p)
            epilogues[j](u_scr[base + i % 2, part_rows[p], :], p)


def _in_proj(x2, g_pre, w_in, tables, lb_f, lb_b, seq_len, tm):
    n = x2.shape[0]
    n_pos_tiles = seq_len // tm
    row = lambda i: (i, 0)
    const = lambda i: (0, 0)
    rope_base, rope_delta = tables
    specs = [pl.BlockSpec((tm // dil, 3 * dil * GROUP_W), row) for _, dil in DILATED_BRANCHES]
    shapes = [jax.ShapeDtypeStruct((n // dil, 3 * dil * GROUP_W), BF16) for _, dil in DILATED_BRANCHES]
    for fields, dtype in ((HG16_FIELDS, BF16), (HG32_FIELDS, F32)):
        specs.append(pl.BlockSpec((tm, len(fields) * GROUP_W), row))
        shapes.append(jax.ShapeDtypeStruct((n, len(fields) * GROUP_W), dtype))
    return pl.pallas_call(
        functools.partial(_in_proj_kernel, tm=tm),
        grid=(n // tm,),
        in_specs=[
            pl.BlockSpec((tm, D_MODEL), row),
            pl.BlockSpec((1, D_MODEL), const),
            pl.BlockSpec((D_MODEL, N_GROUPS * GROUP_W), const, pipeline_mode=pl.Buffered(1)),
            pl.BlockSpec((None, SUBLANES, LANES), lambda i: (i % n_pos_tiles, 0, 0)),
            pl.BlockSpec(rope_delta.shape, lambda i: (0, 0, 0), pipeline_mode=pl.Buffered(1)),
            pl.BlockSpec(lb_f.shape, const),
            pl.BlockSpec(lb_b.shape, const),
        ],
        out_specs=specs,
        out_shape=shapes,
        scratch_shapes=[pltpu.VMEM((2, tm, GROUP_W), F32)]
                       + [pltpu.VMEM((dil * GROUP_W // LANES, tm // dil, LANES), F32)
                          for _, dil in DILATED_BRANCHES[:-1]],
        compiler_params=pltpu.CompilerParams(
            dimension_semantics=("parallel",), vmem_limit_bytes=VMEM_LIMIT),
    )(x2, g_pre, w_in, rope_base, rope_delta, lb_f, lb_b)


def _rope_tables(seq_len, tm):
    half = ROT_DIM // 2
    dim = np.arange(LANES) % ATT_HEAD_DIM
    first, second = dim < half, (dim >= half) & (dim < ROT_DIM)
    rotary = first | second
    inv_freq = ROPE_THETA ** (-jnp.arange(0, ROT_DIM, 2, dtype=F32) / ROT_DIM)
    freq = inv_freq[dim % half][None, :]
    start = jnp.arange(0, seq_len, tm, dtype=F32)[:, None] * freq
    offset = jnp.arange(tm, dtype=F32)[:, None] * freq
    pad = jnp.zeros((seq_len // tm, SUBLANES - 2, LANES), F32)
    base = jnp.concatenate([jnp.cos(start)[:, None], jnp.sin(start)[:, None], pad], axis=1)
    cd, sd = jnp.cos(offset), jnp.sin(offset)
    zero = jnp.zeros_like(cd)
    delta = jnp.stack([
        jnp.where(rotary, cd, zero), jnp.where(rotary, sd, zero), jnp.where(rotary, zero, 1.0),
        jnp.where(first, -cd, zero), jnp.where(first, -sd, zero),
        jnp.where(second, cd, zero), jnp.where(second, sd, zero)])
    return base, delta


def _stat_lane(head):
    return (head % 2) * ATT_HEAD_DIM + head


def _attn_kernel(q_ref, kp_ref, kc_ref, kn_ref, vp_ref, vc_ref, vn_ref, o_ref, st_ref, *scratch,
                 sub_len, tqs, dil, rps):
    stage = scratch[0] if dil > 1 else None
    n = pl.program_id(1)
    n_sub = tqs // ATT_TQ
    heads_per_group = LANES // ATT_HEAD_DIM
    rows = heads_per_group * ATT_TQ
    qi = lax.broadcasted_iota(jnp.int32, (rows, ATT_TK), 0) % ATT_TQ
    kj = lax.broadcasted_iota(jnp.int32, (rows, ATT_TK), 1)
    band_bias = jnp.where((kj >= qi) & (kj <= qi + 2 * ATT_HALF), 0.0, NEG_FILL)
    key_col = lax.broadcasted_iota(jnp.int32, (1, ATT_TK), 1)
    lane = lax.broadcasted_iota(jnp.int32, (ATT_TQ, LANES), 1)
    first_head = lane < ATT_HEAD_DIM
    stat_lane = [(lane == _stat_lane(heads_per_group * g)) | (lane == _stat_lane(heads_per_group * g + 1))
                 for g in range(ATT_WIDTH // LANES)]
    ones = jnp.ones((ATT_TK, LANES), BF16)

    def window(prev_ref, cur_ref, next_ref, i, cols):
        lo, hi = i * ATT_TQ - ATT_HALF, (i + 1) * ATT_TQ + ATT_HALF
        parts = []
        if lo < 0:
            parts.append(prev_ref[:, cols])
        parts.append(cur_ref[max(lo, 0):min(hi, tqs), cols])
        if hi > tqs:
            parts.append(next_ref[:, cols])
        return parts[0] if len(parts) == 1 else jnp.concatenate(parts, axis=0)

    for j in range(rps):
        res = pl.program_id(2) * rps + j
        for i in range(n_sub):
            key0 = n * tqs + i * ATT_TQ - ATT_HALF
            bias = band_bias
            if i == 0:
                bias = bias + jnp.where(key_col >= -key0, 0.0, NEG_FILL)
            if i == n_sub - 1:
                bias = bias + jnp.where(key_col < sub_len - key0, 0.0, NEG_FILL)
            if dil == 1:
                rows_out = pl.ds(i * ATT_TQ, ATT_TQ)
            else:
                rows_out = pl.ds(i * ATT_TQ * dil + res, ATT_TQ, stride=dil)
            stats = jnp.zeros((ATT_TQ, LANES), F32)
            for g in range(ATT_WIDTH // LANES):
                cols = slice(j * ATT_WIDTH + g * LANES, j * ATT_WIDTH + (g + 1) * LANES)
                q2 = q_ref[i * ATT_TQ:(i + 1) * ATT_TQ, cols]
                zero = jnp.zeros_like(q2)
                qs = jnp.concatenate(
                    [jnp.where(first_head, q2, zero), jnp.where(first_head, zero, q2)], axis=0)
                kk = window(kp_ref, kc_ref, kn_ref, i, cols)
                s = lax.dot_general(qs, kk, (((1,), (1,)), ((), ())), preferred_element_type=F32) + bias
                m = jnp.max(s, axis=-1, keepdims=True)
                p = jnp.exp2((s - m).astype(BF16))
                vext = jnp.concatenate([window(vp_ref, vc_ref, vn_ref, i, cols), ones], axis=1)
                r = jnp.dot(p, vext, preferred_element_type=F32)
                o = jnp.where(first_head, r[:ATT_TQ, :LANES], r[ATT_TQ:, :LANES])
                l = jnp.where(first_head, r[:ATT_TQ, LANES:], r[ATT_TQ:, LANES:])
                if dil == 1:
                    o_ref[rows_out, g * LANES:(g + 1) * LANES] = (o / l).astype(o_ref.dtype)
                else:
                    stage[g, rows_out, :] = o / l
                lse2 = jnp.where(first_head, m[:ATT_TQ], m[ATT_TQ:]) + jnp.log2(l)
                stats = jnp.where(stat_lane[g], lse2, stats)
            st_ref[rows_out, :] = stats

    if dil > 1:
        @pl.when(pl.program_id(2) == pl.num_programs(2) - 1)
        def _():
            for g in range(ATT_WIDTH // LANES):
                o_ref[:, g * LANES:(g + 1) * LANES] = stage[g].astype(o_ref.dtype)


def _attn_branch(qkv, batch, seq_len, dil):
    sub_len = seq_len // dil
    t_pos = min(seq_len, ATT_POS_PER_STEP[dil > 1])
    tqs = t_pos // dil
    rps = max(1, min(dil, ATT_QUERIES_PER_STEP // tqs))
    assert seq_len % t_pos == 0 and tqs % ATT_TQ == 0 and dil % rps == 0
    halo_per_tile = tqs // ATT_HALF
    n_halo = sub_len // ATT_HALF
    n_blocks = dil // rps
    qkv = qkv.reshape(batch, sub_len, 3 * dil * ATT_WIDTH)

    def specs(which):
        cur = lambda b, n, r: (b, n, which * n_blocks + r)
        prev = lambda b, n, r: (b, jnp.maximum(n * halo_per_tile - 1, 0), which * n_blocks + r)
        nxt = lambda b, n, r: (b, jnp.minimum((n + 1) * halo_per_tile, n_halo - 1), which * n_blocks + r)
        return (pl.BlockSpec((None, ATT_HALF, rps * ATT_WIDTH), prev),
                pl.BlockSpec((None, tqs, rps * ATT_WIDTH), cur),
                pl.BlockSpec((None, ATT_HALF, rps * ATT_WIDTH), nxt))

    slabs = ATT_WIDTH // LANES
    return pl.pallas_call(
        functools.partial(_attn_kernel, sub_len=sub_len, tqs=tqs, dil=dil, rps=rps),
        grid=(batch, seq_len // t_pos, dil // rps),
        in_specs=[specs(0)[1], *specs(1), *specs(2)],
        out_specs=[pl.BlockSpec((None, t_pos, ATT_WIDTH), lambda b, n, r: (b, n, 0)),
                   pl.BlockSpec((None, t_pos, LANES), lambda b, n, r: (b, n, 0))],
        out_shape=[jax.ShapeDtypeStruct((batch, seq_len, ATT_WIDTH), BF16),
                   jax.ShapeDtypeStruct((batch, seq_len, LANES), F32)],
        scratch_shapes=[pltpu.VMEM((slabs, t_pos, LANES), F32)] if dil > 1 else [],
        compiler_params=pltpu.CompilerParams(
            dimension_semantics=("parallel", "parallel", "arbitrary"), vmem_limit_bytes=VMEM_LIMIT),
    )(*([qkv] * 7))


def _hgrn_kernel(tri_f_ref, tri_b_ref,
                 qf_ref, gf_ref, kf_ref, vf_ref, qb_ref, gb_ref, kb_ref, vb_ref,
                 of_ref, ob_ref,
                 st_f, st_b, b_f, b_b, oi_f, oi_b, k32, *, th):
    n_sub = th // HG_SUBTILE
    n_chunks = HG_SUBTILE // HG_CHUNK
    local_rows = [slice(c * HG_CHUNK, (c + 1) * HG_CHUNK) for c in range(n_chunks)]

    @pl.when(pl.program_id(1) == 0)
    def _():
        st_f[...] = jnp.zeros_like(st_f)
        st_b[...] = jnp.zeros_like(st_b)

    def chunk_sums(tri_ref, g_ref, out):
        tri = tri_ref[...].astype(BF16)
        for s in range(n_sub):
            rows = slice(s * HG_SUBTILE, (s + 1) * HG_SUBTILE)
            g = g_ref[rows, :]
            hi = g.astype(BF16)
            lo = (g - hi.astype(F32)).astype(BF16)
            out[rows, :] = (jnp.dot(tri, hi, preferred_element_type=F32)
                            + jnp.dot(tri, lo, preferred_element_type=F32))

    chunk_sums(tri_f_ref, gf_ref, b_f)
    chunk_sums(tri_b_ref, gb_ref, b_b)

    ti = lax.broadcasted_iota(jnp.int32, (HG_CHUNK, HG_CHUNK), 0)
    si = lax.broadcasted_iota(jnp.int32, (HG_CHUNK, HG_CHUNK), 1)
    dirs = (
        (False, qf_ref, kf_ref, vf_ref, b_f, of_ref, st_f, oi_f, tri_f_ref, si <= ti,
         HG_CHUNK // 2 - 1, HG_CHUNK - 1),
        (True, qb_ref, kb_ref, vb_ref, b_b, ob_ref, st_b, oi_b, tri_b_ref, si >= ti,
         HG_CHUNK // 2, 0),
    )
    nt_dims = (((1,), (1,)), ((), ()))
    tn_dims = (((0,), (0,)), ((), ()))

    streams = [(d, h) for d in range(len(dirs)) for h in range(HG_HEADS)]
    for sub in range(n_sub):
        work = {}
        for d, h in streams:
            rev, q_ref, k_ref, v_ref, b, _, _, _, tri_ref, _, anchor, edge = dirs[d]
            row0 = (n_sub - 1 - sub if rev else sub) * HG_SUBTILE
            rows = slice(row0, row0 + HG_SUBTILE)
            cols = slice(h * HG_DK, (h + 1) * HG_DK)
            bt = b[rows, cols]
            mids = [bt[r.start + anchor:r.start + anchor + 1, :] for r in local_rows]
            edges = [bt[r.start + edge:r.start + edge + 1, :] for r in local_rows]
            edge_rows = jnp.concatenate([jnp.broadcast_to(e, (HG_CHUNK, HG_DK)) for e in edges], axis=0)
            qi32 = q_ref[rows, cols].astype(F32) * jnp.exp(bt)
            kd32 = k_ref[rows, cols].astype(F32) * jnp.exp(edge_rows - bt)
            qa = jnp.concatenate([qi32[r] * jnp.exp(-m) for r, m in zip(local_rows, mids)], axis=0)
            ka = jnp.concatenate(
                [kd32[r] * jnp.exp(m - e) for r, m, e in zip(local_rows, mids, edges)], axis=0)
            a = lax.dot_general(qa.astype(BF16), ka.astype(BF16), nt_dims, preferred_element_type=F32)
            pair_ok = tri_ref[...] > 0.5
            work[d, h] = dict(a=jnp.where(pair_ok, a, 0.0).astype(BF16), edges=edges, row0=row0,
                              qi=qi32.astype(BF16), kd=kd32.astype(BF16), v=v_ref[rows, cols], cols=cols)
        for key in streams:
            w = work[key]
            w["o_intra"] = jnp.dot(w["a"], w["v"], preferred_element_type=F32)
        for step in range(n_chunks):
            for d, h in streams:
                rev, _, _, _, _, o_ref, st, oi, _, _, _, _ = dirs[d]
                w = work[d, h]
                c = n_chunks - 1 - step if rev else step
                r = local_rows[c]
                out_rows = slice(w["row0"] + r.start, w["row0"] + r.stop)
                state = st[h]
                o_inter = lax.dot_general(w["qi"][r], state.astype(BF16), nt_dims,
                                          preferred_element_type=F32)
                oi[out_rows, w["cols"]] = o_inter
                o_ref[out_rows, w["cols"]] = (o_inter + w["o_intra"][r]).astype(o_ref.dtype)
                st[h] = state * jnp.exp(w["edges"][c]) + lax.dot_general(
                    w["v"][r], w["kd"][r], tn_dims, preferred_element_type=F32)

    q_peak = jnp.maximum(jnp.max(jnp.abs(qf_ref[...].astype(F32))), jnp.max(jnp.abs(qb_ref[...].astype(F32))))
    safe = ((jnp.minimum(jnp.min(b_f[...]), jnp.min(b_b[...])) >= -SAFE_DECAY_LOG)
            & (q_peak <= SAFE_Q_PEAK))

    @pl.when(jnp.logical_not(safe))
    def _():
        for _, q_ref, k_ref, v_ref, b, o_ref, _, oi, _, mask, _, _ in dirs:
            k32[...] = k_ref[...].astype(F32)
            for h in range(HG_HEADS):
                cols = slice(h * HG_DK, (h + 1) * HG_DK)

                def per_chunk(c, carry):
                    r0 = pl.multiple_of(c * HG_CHUNK, HG_CHUNK)
                    bc = b[pl.ds(r0, HG_CHUNK), cols]
                    qc = q_ref[pl.ds(r0, HG_CHUNK), cols].astype(F32)

                    def per_key_group(s8, a):
                        k0 = pl.multiple_of(r0 + s8 * SUBLANES, SUBLANES)
                        b8 = b[pl.ds(k0, SUBLANES), cols]
                        k8 = k32[pl.ds(k0, SUBLANES), cols]
                        for j in range(SUBLANES):
                            w = qc * k8[j:j + 1, :] * jnp.exp(jnp.minimum(bc - b8[j:j + 1, :], 0.0))
                            a = jnp.where(si == s8 * SUBLANES + j, jnp.sum(w, axis=-1, keepdims=True), a)
                        return a

                    a = lax.fori_loop(0, HG_CHUNK // SUBLANES, per_key_group,
                                      jnp.zeros((HG_CHUNK, HG_CHUNK), F32))
                    a = jnp.where(mask, a, 0.0).astype(BF16)
                    o = oi[pl.ds(r0, HG_CHUNK), cols] + jnp.dot(
                        a, v_ref[pl.ds(r0, HG_CHUNK), cols], preferred_element_type=F32)
                    o_ref[pl.ds(r0, HG_CHUNK), cols] = o.astype(o_ref.dtype)
                    return carry

                lax.fori_loop(0, th // HG_CHUNK, per_chunk, 0)


def _block_triangular(th, upper):
    t = np.arange(th)
    same = (t[:, None] // HG_CHUNK) == (t[None, :] // HG_CHUNK)
    tri = (t[None, :] >= t[:, None]) if upper else (t[None, :] <= t[:, None])
    return jnp.asarray(same & tri, dtype=F32)


def _hgrn(hg16, hg32, batch, seq_len, th):
    nt = seq_len // th
    hg16 = hg16.reshape(batch, seq_len, len(HG16_FIELDS) * GROUP_W)
    hg32 = hg32.reshape(batch, seq_len, len(HG32_FIELDS) * GROUP_W)
    const = lambda b, i: (0, 0)

    def tile(fields, name, rev):
        j = fields.index(name)
        index = (lambda b, i: (b, nt - 1 - i, j)) if rev else (lambda b, i: (b, i, j))
        return pl.BlockSpec((None, th, HG_WIDTH), index)

    tile_f = pl.BlockSpec((None, th, HG_WIDTH), lambda b, i: (b, i, 0))
    tile_b = pl.BlockSpec((None, th, HG_WIDTH), lambda b, i: (b, nt - 1 - i, 0))
    tri = pl.BlockSpec((HG_SUBTILE, HG_SUBTILE), const)
    tile_f32 = pltpu.VMEM((th, HG_WIDTH), F32)
    return pl.pallas_call(
        functools.partial(_hgrn_kernel, th=th),
        grid=(batch, nt),
        in_specs=[tri, tri,
                  tile(HG16_FIELDS, "qh", False), tile(HG32_FIELDS, "gf", False),
                  tile(HG16_FIELDS, "kf", False), tile(HG16_FIELDS, "vh", False),
                  tile(HG16_FIELDS, "qh", True), tile(HG32_FIELDS, "gb", True),
                  tile(HG16_FIELDS, "kb", True), tile(HG16_FIELDS, "vh", True)],
        out_specs=[tile_f, tile_b],
        out_shape=[jax.ShapeDtypeStruct((batch, seq_len, HG_WIDTH), BF16)] * 2,
        scratch_shapes=[
            pltpu.VMEM((HG_HEADS, HG_DK, HG_DK), F32),
            pltpu.VMEM((HG_HEADS, HG_DK, HG_DK), F32),
            tile_f32, tile_f32,
            tile_f32, tile_f32,
            tile_f32,
        ],
        compiler_params=pltpu.CompilerParams(
            dimension_semantics=("parallel", "arbitrary"), vmem_limit_bytes=VMEM_LIMIT),
    )(_block_triangular(HG_SUBTILE, False), _block_triangular(HG_SUBTILE, True),
      hg16, hg32, hg16, hg16, hg16, hg32, hg16, hg16)


def _out_ffn_kernel(x_ref, o1_ref, o2_ref, o3_ref, s1_ref, s2_ref, s3_ref, hf_ref, hb_ref, gs_ref,
                    expand_ref, gh_ref, wout_ref, gpm_ref, gpf_ref, gpo_ref, wg_ref, wu_ref, wd_ref,
                    y_ref):
    def mixer_output(rows):
        lses = (s1_ref[rows, :], s2_ref[rows, :], s3_ref[rows, :])
        top = jnp.maximum(jnp.maximum(lses[0], lses[1]), lses[2])
        es = [jnp.exp2(s - top) for s in lses]
        den = es[0] + es[1] + es[2]
        att = None
        for e, o_ref in zip(es, (o1_ref, o2_ref, o3_ref)):
            w = jnp.dot((e / den).astype(BF16), expand_ref[...], preferred_element_type=F32)
            term = w * o_ref[rows, :].astype(F32)
            att = term if att is None else att + term
        o = hf_ref[rows, :].astype(F32) + hb_ref[rows, :].astype(F32)
        hg = jnp.concatenate(
            [_rms(o[:, h * HG_DK:(h + 1) * HG_DK]) * gh_ref[...] for h in range(HG_HEADS)], axis=1)
        hg = hg * gs_ref[rows, :].astype(F32)
        return jnp.concatenate([att, hg], axis=1).astype(BF16)

    tm = x_ref.shape[0]
    halves = [slice(i * (tm // 2), (i + 1) * (tm // 2)) for i in range(2)]
    dot = functools.partial(jnp.dot, preferred_element_type=F32)
    mix_in = [mixer_output(r) for r in halves]
    mix = [dot(m, wout_ref[...]) for m in mix_in]
    x1 = [x_ref[r, :] + _rms(m) * gpm_ref[...] for r, m in zip(halves, mix)]
    h2 = [(_rms(t) * gpf_ref[...]).astype(BF16) for t in x1]
    gate = [dot(t, wg_ref[...]) for t in h2]
    up = [dot(t, wu_ref[...]) for t in h2]
    act = [(_silu(g) * u).astype(BF16) for g, u in zip(gate, up)]
    ff = [dot(t, wd_ref[...]) for t in act]
    for r, t, f in zip(halves, x1, ff):
        y_ref[r, :] = t + _rms(f) * gpo_ref[...]


def _out_ffn(x, att_o, att_s, hf, hb, gs, g_hnorm, w_out, g_pm, g_pf, g_po, w_gate, w_up, w_down, tm):
    batch, seq_len, _ = x.shape
    col_head = np.arange(ATT_WIDTH) // ATT_HEAD_DIM
    expand = np.zeros((LANES, ATT_WIDTH), np.float32)
    expand[_stat_lane(col_head), np.arange(ATT_WIDTH)] = 1.0
    expand = jnp.asarray(expand, dtype=BF16)
    const = lambda b, i: (0, 0)
    tile = lambda w: pl.BlockSpec((None, tm, w), lambda b, i: (b, i, 0))
    whole = lambda a: pl.BlockSpec(a.shape, const, pipeline_mode=pl.Buffered(1))
    consts = (expand, g_hnorm, w_out, g_pm, g_pf, g_po, w_gate, w_up, w_down)
    return pl.pallas_call(
        _out_ffn_kernel,
        grid=(batch, seq_len // tm),
        in_specs=[tile(D_MODEL)] + [tile(ATT_WIDTH)] * 3 + [tile(LANES)] * 3 + [tile(HG_WIDTH)] * 2
                 + [pl.BlockSpec((None, tm, GROUP_W), lambda b, i: (b, i, HG16_FIELDS.index("gs")))]
                 + [whole(a) for a in consts],
        out_specs=tile(D_MODEL),
        out_shape=jax.ShapeDtypeStruct((batch, seq_len, D_MODEL), F32),
        compiler_params=pltpu.CompilerParams(
            dimension_semantics=("parallel", "parallel"), vmem_limit_bytes=VMEM_LIMIT),
    )(x, *att_o, *att_s, hf, hb, gs, *consts)


def _layer(x, tables, w_in, w_out, lb_fwd, lb_bwd, g_hnorm, g_pre_mix, g_post_mix, g_pre_ffn, g_post_ffn,
           w_gate, w_up, w_down, *, tm_in=IN_PROJ_ROWS, th=HG_TILE_ROWS, tm_out=OUT_FFN_ROWS):
    batch, seq_len, _ = x.shape
    x2 = x.reshape(batch * seq_len, D_MODEL)
    *qkv, hg16, hg32 = _in_proj(x2, g_pre_mix, w_in, tables, lb_fwd, lb_bwd, seq_len, tm_in)
    att = [_attn_branch(t, batch, seq_len, dil) for t, (_, dil) in zip(qkv, DILATED_BRANCHES)]
    hf, hb = _hgrn(hg16, hg32, batch, seq_len, th)
    return _out_ffn(x, [o for o, _ in att], [s for _, s in att], hf, hb,
                    hg16.reshape(batch, seq_len, len(HG16_FIELDS) * GROUP_W), g_hnorm, w_out,
                    g_post_mix, g_pre_ffn, g_post_ffn, w_gate, w_up, w_down, tm_out)


def kernel(x_prompt, x_sample, w_in, w_out, lb_fwd, lb_bwd, g_hgrn_norm, g_pre_mix, g_post_mix,
           g_pre_ffn, g_post_ffn, w_gate, w_up, w_down):
    assert w_in.shape[0] == 1, "one layer"
    assert all(w // (2 * d) == ATT_HALF for w, d in DILATED_BRANCHES)
    params = (w_in[0].astype(BF16), w_out[0].astype(BF16), lb_fwd, lb_bwd, g_hgrn_norm,
              g_pre_mix, g_post_mix, g_pre_ffn, g_post_ffn,
              w_gate[0].astype(BF16), w_up[0].astype(BF16), w_down[0].astype(BF16))
    tables = _rope_tables(max(x_prompt.shape[1], x_sample.shape[1]), IN_PROJ_ROWS)
    return _layer(x_prompt, tables, *params), _layer(x_sample, tables, *params)
```

```python
import functools
import math

import jax
import jax.numpy as jnp
import numpy as np
from jax import lax
from jax.experimental import pallas as pl
from jax.experimental.pallas import tpu as pltpu

F32 = jnp.float32
BF16 = jnp.bfloat16

D_MODEL = 1024
ATT_HEADS = 8
ATT_HEAD_DIM = 64
ATT_WIDTH = ATT_HEADS * ATT_HEAD_DIM
DILATED_BRANCHES = ((128, 1), (512, 4), (2048, 16))
ROT_DIM = ATT_HEAD_DIM // 4
ROPE_THETA = 500000.0
HG_HEADS = 4
HG_DK = 128
HG_WIDTH = HG_HEADS * HG_DK
HG_CHUNK = 64
HG_SUBTILE = 256
GROUP_W = 512
N_GROUPS = 8
HG16_FIELDS = ("qh", "kf", "kb", "vh", "gs")
HG32_FIELDS = ("gf", "gb")
NORM_EPS = 1e-6
NEG_FILL = -1e30
LOG2E = math.log2(math.e)

LANES = 128
SUBLANES = 8
ATT_HALF = 64
ATT_TQ = 128
ATT_TK = ATT_TQ + 2 * ATT_HALF
ATT_QUERIES_PER_STEP = 2048
ATT_POS_PER_STEP = (4096, 4096)
IN_PROJ_ROWS = 512
HG_TILE_ROWS = 1024
OUT_FFN_ROWS = 512
SAFE_DECAY_LOG = 80.0
SAFE_Q_PEAK = 1e3
VMEM_LIMIT = 56 * 1024 * 1024


def _sigmoid_pair(z):
    e = jnp.exp(-jnp.abs(z))
    big = 1.0 / (1.0 + e)
    small = e * big
    pos = z >= 0
    return jnp.where(pos, big, small), jnp.where(pos, small, big)


def _silu(z):
    s, _ = _sigmoid_pair(z)
    return z * s


def _rms(x):
    return x * lax.rsqrt(jnp.mean(x * x, axis=-1, keepdims=True) + NORM_EPS)


def _in_proj_kernel(x_ref, g_ref, w_ref, rope_base_ref, rope_delta_ref, lbf_ref, lbb_ref, *refs, tm):
    n_br = len(DILATED_BRANCHES)
    att_refs = refs[:n_br]
    hg16_ref, hg32_ref = refs[n_br:n_br + 2]
    u_scr = refs[n_br + 2]
    stages = refs[n_br + 3:]
    hg16 = lambda name: slice(HG16_FIELDS.index(name) * GROUP_W, (HG16_FIELDS.index(name) + 1) * GROUP_W)
    hg32 = lambda name: slice(HG32_FIELDS.index(name) * GROUP_W, (HG32_FIELDS.index(name) + 1) * GROUP_W)
    slabs = GROUP_W // LANES
    n_parts = 1
    part = tm // n_parts
    part_rows = [slice(p * part, (p + 1) * part) for p in range(n_parts)]
    h = [(_rms(x_ref[r, :]) * g_ref[...]).astype(BF16) for r in part_rows]

    def emit_dilated(t, which, p):
        assert DILATED_BRANCHES[0][1] == 1
        att_refs[0][part_rows[p], which * GROUP_W:(which + 1) * GROUP_W] = t.astype(BF16)
        for s in range(slabs):
            stages[0][s, part_rows[p], :] = t[:, s * LANES:(s + 1) * LANES]
        for b in range(1, n_br):
            dil, prev = DILATED_BRANCHES[b][1], DILATED_BRANCHES[b - 1][1]
            step = dil // prev
            out = att_refs[b]
            rows = slice(p * part // dil, (p + 1) * part // dil)
            for r in range(dil):
                r_prev, r_new = r % prev, r // prev
                for s in range(slabs):
                    piece = stages[b - 1][r_prev * slabs + s,
                                          pl.ds(r_new + step * rows.start, part // dil, stride=step), :]
                    if b + 1 < n_br:
                        stages[b][r * slabs + s, rows, :] = piece
                    lane0 = (which * dil + r) * GROUP_W + s * LANES
                    out[rows, lane0:lane0 + LANES] = piece.astype(BF16)

    def proj(j, p):
        return jnp.dot(h[p], w_ref[:, j * GROUP_W:(j + 1) * GROUP_W], preferred_element_type=F32)

    cb, sb = rope_base_ref[0:1, :], rope_base_ref[1:2, :]
    dt = rope_delta_ref
    reps = GROUP_W // LANES
    half = ROT_DIM // 2

    def rope(t, p):
        r = part_rows[p]
        cos = jnp.concatenate([cb * dt[0, r, :] - sb * dt[1, r, :] + dt[2, r, :]] * reps, axis=1)
        sina = jnp.concatenate([sb * dt[3, r, :] + cb * dt[4, r, :]] * reps, axis=1)
        sinb = jnp.concatenate([sb * dt[5, r, :] + cb * dt[6, r, :]] * reps, axis=1)
        return t * cos + pltpu.roll(t, GROUP_W - half, 1) * sina + pltpu.roll(t, half, 1) * sinb

    def lower_bound(lb_ref):
        a = lb_ref[...]
        e = jnp.exp(a - jnp.max(a, axis=0, keepdims=True))
        return e[0:1, :] / jnp.sum(e, axis=0, keepdims=True)

    def gates(z, lb, g_name, k_name, p):
        s_pos, s_neg = _sigmoid_pair(z)
        hg32_ref[part_rows[p], hg32(g_name)] = jnp.log(lb + (1.0 - lb) * s_pos)
        hg16_ref[part_rows[p], hg16(k_name)] = ((1.0 - lb) * s_neg).astype(BF16)

    def store_silu(name):
        def epilogue(t, p):
            hg16_ref[part_rows[p], hg16(name)] = _silu(t).astype(BF16)
        return epilogue

    def store_plain(t, p):
        hg16_ref[part_rows[p], hg16("vh")] = t.astype(BF16)

    epilogues = (
        lambda t, p: emit_dilated(rope(t, p) * (ATT_HEAD_DIM ** -0.5 * LOG2E), 0, p),
        lambda t, p: emit_dilated(rope(t, p), 1, p),
        lambda t, p: emit_dilated(t, 2, p),
        store_silu("qh"),
        lambda t, p: gates(t, lower_bound(lbf_ref), "gf", "kf", p),
        lambda t, p: gates(t, lower_bound(lbb_ref), "gb", "kb", p),
        store_plain,
        store_silu("gs"),
    )

    base = jnp.minimum(pl.program_id(0), 0)
    order = (0, 1, 2, 3, 4, 5, 7, 6)
    for p in range(n_parts):
        u_scr[base, part_rows[p], :] = proj(order[0], p)
    for i, j in enumerate(order):
        for p in range(n_parts):
            if i + 1 < N_GROUPS:
                u_scr[base + (i + 1) % 2, part_rows[p], :] = proj(order[i + 1], p)
            epilogues[j](u_scr[base + i % 2, part_rows[p], :], p)


def _in_proj(x2, g_pre, w_in, tables, lb_f, lb_b, seq_len, tm):
    n = x2.shape[0]
    n_pos_tiles = seq_len // tm
    row = lambda i: (i, 0)
    const = lambda i: (0, 0)
    rope_base, rope_delta = tables
    specs = [pl.BlockSpec((tm // dil, 3 * dil * GROUP_W), row) for _, dil in DILATED_BRANCHES]
    shapes = [jax.ShapeDtypeStruct((n // dil, 3 * dil * GROUP_W), BF16) for _, dil in DILATED_BRANCHES]
    for fields, dtype in ((HG16_FIELDS, BF16), (HG32_FIELDS, F32)):
        specs.append(pl.BlockSpec((tm, len(fields) * GROUP_W), row))
        shapes.append(jax.ShapeDtypeStruct((n, len(fields) * GROUP_W), dtype))
    return pl.pallas_call(
        functools.partial(_in_proj_kernel, tm=tm),
        grid=(n // tm,),
        in_specs=[
            pl.BlockSpec((tm, D_MODEL), row),
            pl.BlockSpec((1, D_MODEL), const),
            pl.BlockSpec((D_MODEL, N_GROUPS * GROUP_W), const, pipeline_mode=pl.Buffered(1)),
            pl.BlockSpec((None, SUBLANES, LANES), lambda i: (i % n_pos_tiles, 0, 0)),
            pl.BlockSpec(rope_delta.shape, lambda i: (0, 0, 0), pipeline_mode=pl.Buffered(1)),
            pl.BlockSpec(lb_f.shape, const),
            pl.BlockSpec(lb_b.shape, const),
        ],
        out_specs=specs,
        out_shape=shapes,
        scratch_shapes=[pltpu.VMEM((2, tm, GROUP_W), F32)]
                       + [pltpu.VMEM((dil * GROUP_W // LANES, tm // dil, LANES), F32)
                          for _, dil in DILATED_BRANCHES[:-1]],
        compiler_params=pltpu.CompilerParams(
            dimension_semantics=("parallel",), vmem_limit_bytes=VMEM_LIMIT),
    )(x2, g_pre, w_in, rope_base, rope_delta, lb_f, lb_b)


def _rope_tables(seq_len, tm):
    half = ROT_DIM // 2
    dim = np.arange(LANES) % ATT_HEAD_DIM
    first, second = dim < half, (dim >= half) & (dim < ROT_DIM)
    rotary = first | second
    inv_freq = ROPE_THETA ** (-jnp.arange(0, ROT_DIM, 2, dtype=F32) / ROT_DIM)
    freq = inv_freq[dim % half][None, :]
    start = jnp.arange(0, seq_len, tm, dtype=F32)[:, None] * freq
    offset = jnp.arange(tm, dtype=F32)[:, None] * freq
    pad = jnp.zeros((seq_len // tm, SUBLANES - 2, LANES), F32)
    base = jnp.concatenate([jnp.cos(start)[:, None], jnp.sin(start)[:, None], pad], axis=1)
    cd, sd = jnp.cos(offset), jnp.sin(offset)
    zero = jnp.zeros_like(cd)
    delta = jnp.stack([
        jnp.where(rotary, cd, zero), jnp.where(rotary, sd, zero), jnp.where(rotary, zero, 1.0),
        jnp.where(first, -cd, zero), jnp.where(first, -sd, zero),
        jnp.where(second, cd, zero), jnp.where(second, sd, zero)])
    return base, delta


def _stat_lane(head):
    return (head % 2) * ATT_HEAD_DIM + head


def _attn_kernel(q_ref, kp_ref, kc_ref, kn_ref, vp_ref, vc_ref, vn_ref, o_ref, st_ref, *scratch,
                 sub_len, tqs, dil, rps):
    stage = scratch[0] if dil > 1 else None
    n = pl.program_id(1)
    n_sub = tqs // ATT_TQ
    heads_per_group = LANES // ATT_HEAD_DIM
    rows = heads_per_group * ATT_TQ
    qi = lax.broadcasted_iota(jnp.int32, (rows, ATT_TK), 0) % ATT_TQ
    kj = lax.broadcasted_iota(jnp.int32, (rows, ATT_TK), 1)
    band_bias = jnp.where((kj >= qi) & (kj <= qi + 2 * ATT_HALF), 0.0, NEG_FILL)
    key_col = lax.broadcasted_iota(jnp.int32, (1, ATT_TK), 1)
    lane = lax.broadcasted_iota(jnp.int32, (ATT_TQ, LANES), 1)
    first_head = lane < ATT_HEAD_DIM
    stat_lane = [(lane == _stat_lane(heads_per_group * g)) | (lane == _stat_lane(heads_per_group * g + 1))
                 for g in range(ATT_WIDTH // LANES)]
    ones = jnp.ones((ATT_TK, LANES), BF16)

    def window(prev_ref, cur_ref, next_ref, i, cols):
        lo, hi = i * ATT_TQ - ATT_HALF, (i + 1) * ATT_TQ + ATT_HALF
        parts = []
        if lo < 0:
            parts.append(prev_ref[:, cols])
        parts.append(cur_ref[max(lo, 0):min(hi, tqs), cols])
        if hi > tqs:
            parts.append(next_ref[:, cols])
        return parts[0] if len(parts) == 1 else jnp.concatenate(parts, axis=0)

    for j in range(rps):
        res = pl.program_id(2) * rps + j
        for i in range(n_sub):
            key0 = n * tqs + i * ATT_TQ - ATT_HALF
            bias = band_bias
            if i == 0:
                bias = bias + jnp.where(key_col >= -key0, 0.0, NEG_FILL)
            if i == n_sub - 1:
                bias = bias + jnp.where(key_col < sub_len - key0, 0.0, NEG_FILL)
            if dil == 1:
                rows_out = pl.ds(i * ATT_TQ, ATT_TQ)
            else:
                rows_out = pl.ds(i * ATT_TQ * dil + res, ATT_TQ, stride=dil)
            stats = jnp.zeros((ATT_TQ, LANES), F32)
            for g in range(ATT_WIDTH // LANES):
                cols = slice(j * ATT_WIDTH + g * LANES, j * ATT_WIDTH + (g + 1) * LANES)
                q2 = q_ref[i * ATT_TQ:(i + 1) * ATT_TQ, cols]
                zero = jnp.zeros_like(q2)
                qs = jnp.concatenate(
                    [jnp.where(first_head, q2, zero), jnp.where(first_head, zero, q2)], axis=0)
                kk = window(kp_ref, kc_ref, kn_ref, i, cols)
                s = lax.dot_general(qs, kk, (((1,), (1,)), ((), ())), preferred_element_type=F32) + bias
                m = jnp.max(s, axis=-1, keepdims=True)
                p = jnp.exp2((s - m).astype(BF16))
                vext = jnp.concatenate([window(vp_ref, vc_ref, vn_ref, i, cols), ones], axis=1)
                r = jnp.dot(p, vext, preferred_element_type=F32)
                o = jnp.where(first_head, r[:ATT_TQ, :LANES], r[ATT_TQ:, :LANES])
                l = jnp.where(first_head, r[:ATT_TQ, LANES:], r[ATT_TQ:, LANES:])
                if dil == 1:
                    o_ref[rows_out, g * LANES:(g + 1) * LANES] = (o / l).astype(o_ref.dtype)
                else:
                    stage[g, rows_out, :] = o / l
                lse2 = jnp.where(first_head, m[:ATT_TQ], m[ATT_TQ:]) + jnp.log2(l)
                stats = jnp.where(stat_lane[g], lse2, stats)
            st_ref[rows_out, :] = stats

    if dil > 1:
        @pl.when(pl.program_id(2) == pl.num_programs(2) - 1)
        def _():
            for g in range(ATT_WIDTH // LANES):
                o_ref[:, g * LANES:(g + 1) * LANES] = stage[g].astype(o_ref.dtype)


def _attn_branch(qkv, batch, seq_len, dil):
    sub_len = seq_len // dil
    t_pos = min(seq_len, ATT_POS_PER_STEP[dil > 1])
    tqs = t_pos // dil
    rps = max(1, min(dil, ATT_QUERIES_PER_STEP // tqs))
    assert seq_len % t_pos == 0 and tqs % ATT_TQ == 0 and dil % rps == 0
    halo_per_tile = tqs // ATT_HALF
    n_halo = sub_len // ATT_HALF
    n_blocks = dil // rps
    qkv = qkv.reshape(batch, sub_len, 3 * dil * ATT_WIDTH)

    def specs(which):
        cur = lambda b, n, r: (b, n, which * n_blocks + r)
        prev = lambda b, n, r: (b, jnp.maximum(n * halo_per_tile - 1, 0), which * n_blocks + r)
        nxt = lambda b, n, r: (b, jnp.minimum((n + 1) * halo_per_tile, n_halo - 1), which * n_blocks + r)
        return (pl.BlockSpec((None, ATT_HALF, rps * ATT_WIDTH), prev),
                pl.BlockSpec((None, tqs, rps * ATT_WIDTH), cur),
                pl.BlockSpec((None, ATT_HALF, rps * ATT_WIDTH), nxt))

    slabs = ATT_WIDTH // LANES
    return pl.pallas_call(
        functools.partial(_attn_kernel, sub_len=sub_len, tqs=tqs, dil=dil, rps=rps),
        grid=(batch, seq_len // t_pos, dil // rps),
        in_specs=[specs(0)[1], *specs(1), *specs(2)],
        out_specs=[pl.BlockSpec((None, t_pos, ATT_WIDTH), lambda b, n, r: (b, n, 0)),
                   pl.BlockSpec((None, t_pos, LANES), lambda b, n, r: (b, n, 0))],
        out_shape=[jax.ShapeDtypeStruct((batch, seq_len, ATT_WIDTH), BF16),
                   jax.ShapeDtypeStruct((batch, seq_len, LANES), F32)],
        scratch_shapes=[pltpu.VMEM((slabs, t_pos, LANES), F32)] if dil > 1 else [],
        compiler_params=pltpu.CompilerParams(
            dimension_semantics=("parallel", "parallel", "arbitrary"), vmem_limit_bytes=VMEM_LIMIT),
    )(*([qkv] * 7))


def _hgrn_kernel(tri_f_ref, tri_b_ref,
                 qf_ref, gf_ref, kf_ref, vf_ref, qb_ref, gb_ref, kb_ref, vb_ref,
                 of_ref, ob_ref,
                 st_f, st_b, b_f, b_b, oi_f, oi_b, k32, *, th):
    n_sub = th // HG_SUBTILE
    n_chunks = HG_SUBTILE // HG_CHUNK
    local_rows = [slice(c * HG_CHUNK, (c + 1) * HG_CHUNK) for c in range(n_chunks)]

    @pl.when(pl.program_id(1) == 0)
    def _():
        st_f[...] = jnp.zeros_like(st_f)
        st_b[...] = jnp.zeros_like(st_b)

    def chunk_sums(tri_ref, g_ref, out):
        tri = tri_ref[...].astype(BF16)
        for s in range(n_sub):
            rows = slice(s * HG_SUBTILE, (s + 1) * HG_SUBTILE)
            g = g_ref[rows, :]
            hi = g.astype(BF16)
            lo = (g - hi.astype(F32)).astype(BF16)
            out[rows, :] = (jnp.dot(tri, hi, preferred_element_type=F32)
                            + jnp.dot(tri, lo, preferred_element_type=F32))

    chunk_sums(tri_f_ref, gf_ref, b_f)
    chunk_sums(tri_b_ref, gb_ref, b_b)

    ti = lax.broadcasted_iota(jnp.int32, (HG_CHUNK, HG_CHUNK), 0)
    si = lax.broadcasted_iota(jnp.int32, (HG_CHUNK, HG_CHUNK), 1)
    dirs = (
        (False, qf_ref, kf_ref, vf_ref, b_f, of_ref, st_f, oi_f, tri_f_ref, si <= ti,
         HG_CHUNK // 2 - 1, HG_CHUNK - 1),
        (True, qb_ref, kb_ref, vb_ref, b_b, ob_ref, st_b, oi_b, tri_b_ref, si >= ti,
         HG_CHUNK // 2, 0),
    )
    nt_dims = (((1,), (1,)), ((), ()))
    tn_dims = (((0,), (0,)), ((), ()))

    streams = [(d, h) for d in range(len(dirs)) for h in range(HG_HEADS)]
    for sub in range(n_sub):
        work = {}
        for d, h in streams:
            rev, q_ref, k_ref, v_ref, b, _, _, _, tri_ref, _, anchor, edge = dirs[d]
            row0 = (n_sub - 1 - sub if rev else sub) * HG_SUBTILE
            rows = slice(row0, row0 + HG_SUBTILE)
            cols = slice(h * HG_DK, (h + 1) * HG_DK)
            bt = b[rows, cols]
            mids = [bt[r.start + anchor:r.start + anchor + 1, :] for r in local_rows]
            edges = [bt[r.start + edge:r.start + edge + 1, :] for r in local_rows]
            edge_rows = jnp.concatenate([jnp.broadcast_to(e, (HG_CHUNK, HG_DK)) for e in edges], axis=0)
            qi32 = q_ref[rows, cols].astype(F32) * jnp.exp(bt)
            kd32 = k_ref[rows, cols].astype(F32) * jnp.exp(edge_rows - bt)
            qa = jnp.concatenate([qi32[r] * jnp.exp(-m) for r, m in zip(local_rows, mids)], axis=0)
            ka = jnp.concatenate(
                [kd32[r] * jnp.exp(m - e) for r, m, e in zip(local_rows, mids, edges)], axis=0)
            a = lax.dot_general(qa.astype(BF16), ka.astype(BF16), nt_dims, preferred_element_type=F32)
            pair_ok = tri_ref[...] > 0.5
            work[d, h] = dict(a=jnp.where(pair_ok, a, 0.0).astype(BF16), edges=edges, row0=row0,
                              qi=qi32.astype(BF16), kd=kd32.astype(BF16), v=v_ref[rows, cols], cols=cols)
        for key in streams:
            w = work[key]
            w["o_intra"] = jnp.dot(w["a"], w["v"], preferred_element_type=F32)
        for step in range(n_chunks):
            for d, h in streams:
                rev, _, _, _, _, o_ref, st, oi, _, _, _, _ = dirs[d]
                w = work[d, h]
                c = n_chunks - 1 - step if rev else step
                r = local_rows[c]
                out_rows = slice(w["row0"] + r.start, w["row0"] + r.stop)
                state = st[h]
                o_inter = lax.dot_general(w["qi"][r], state.astype(BF16), nt_dims,
                                          preferred_element_type=F32)
                oi[out_rows, w["cols"]] = o_inter
                o_ref[out_rows, w["cols"]] = (o_inter + w["o_intra"][r]).astype(o_ref.dtype)
                st[h] = state * jnp.exp(w["edges"][c]) + lax.dot_general(
                    w["v"][r], w["kd"][r], tn_dims, preferred_element_type=F32)

    q_rows = jnp.maximum(jnp.abs(qf_ref[...]), jnp.abs(qb_ref[...])).reshape(-1, 2 * SUBLANES, HG_WIDTH)
    q_peak = jnp.max(jnp.max(q_rows, axis=0).astype(F32))
    safe = ((jnp.minimum(jnp.min(b_f[...]), jnp.min(b_b[...])) >= -SAFE_DECAY_LOG)
            & (q_peak <= SAFE_Q_PEAK))

    @pl.when(jnp.logical_not(safe))
    def _():
        for _, q_ref, k_ref, v_ref, b, o_ref, _, oi, _, mask, _, _ in dirs:
            k32[...] = k_ref[...].astype(F32)
            for h in range(HG_HEADS):
                cols = slice(h * HG_DK, (h + 1) * HG_DK)

                def per_chunk(c, carry):
                    r0 = pl.multiple_of(c * HG_CHUNK, HG_CHUNK)
                    bc = b[pl.ds(r0, HG_CHUNK), cols]
                    qc = q_ref[pl.ds(r0, HG_CHUNK), cols].astype(F32)

                    def per_key_group(s8, a):
                        k0 = pl.multiple_of(r0 + s8 * SUBLANES, SUBLANES)
                        b8 = b[pl.ds(k0, SUBLANES), cols]
                        k8 = k32[pl.ds(k0, SUBLANES), cols]
                        for j in range(SUBLANES):
                            w = qc * k8[j:j + 1, :] * jnp.exp(jnp.minimum(bc - b8[j:j + 1, :], 0.0))
                            a = jnp.where(si == s8 * SUBLANES + j, jnp.sum(w, axis=-1, keepdims=True), a)
                        return a

                    a = lax.fori_loop(0, HG_CHUNK // SUBLANES, per_key_group,
                                      jnp.zeros((HG_CHUNK, HG_CHUNK), F32))
                    a = jnp.where(mask, a, 0.0).astype(BF16)
                    o = oi[pl.ds(r0, HG_CHUNK), cols] + jnp.dot(
                        a, v_ref[pl.ds(r0, HG_CHUNK), cols], preferred_element_type=F32)
                    o_ref[pl.ds(r0, HG_CHUNK), cols] = o.astype(o_ref.dtype)
                    return carry

                lax.fori_loop(0, th // HG_CHUNK, per_chunk, 0)


def _block_triangular(th, upper):
    t = np.arange(th)
    same = (t[:, None] // HG_CHUNK) == (t[None, :] // HG_CHUNK)
    tri = (t[None, :] >= t[:, None]) if upper else (t[None, :] <= t[:, None])
    return jnp.asarray(same & tri, dtype=F32)


def _hgrn(hg16, hg32, batch, seq_len, th):
    nt = seq_len // th
    hg16 = hg16.reshape(batch, seq_len, len(HG16_FIELDS) * GROUP_W)
    hg32 = hg32.reshape(batch, seq_len, len(HG32_FIELDS) * GROUP_W)
    const = lambda b, i: (0, 0)

    def tile(fields, name, rev):
        j = fields.index(name)
        index = (lambda b, i: (b, nt - 1 - i, j)) if rev else (lambda b, i: (b, i, j))
        return pl.BlockSpec((None, th, HG_WIDTH), index)

    tile_f = pl.BlockSpec((None, th, HG_WIDTH), lambda b, i: (b, i, 0))
    tile_b = pl.BlockSpec((None, th, HG_WIDTH), lambda b, i: (b, nt - 1 - i, 0))
    tri = pl.BlockSpec((HG_SUBTILE, HG_SUBTILE), const)
    tile_f32 = pltpu.VMEM((th, HG_WIDTH), F32)
    return pl.pallas_call(
        functools.partial(_hgrn_kernel, th=th),
        grid=(batch, nt),
        in_specs=[tri, tri,
                  tile(HG16_FIELDS, "qh", False), tile(HG32_FIELDS, "gf", False),
                  tile(HG16_FIELDS, "kf", False), tile(HG16_FIELDS, "vh", False),
                  tile(HG16_FIELDS, "qh", True), tile(HG32_FIELDS, "gb", True),
                  tile(HG16_FIELDS, "kb", True), tile(HG16_FIELDS, "vh", True)],
        out_specs=[tile_f, tile_b],
        out_shape=[jax.ShapeDtypeStruct((batch, seq_len, HG_WIDTH), BF16)] * 2,
        scratch_shapes=[
            pltpu.VMEM((HG_HEADS, HG_DK, HG_DK), F32),
            pltpu.VMEM((HG_HEADS, HG_DK, HG_DK), F32),
            tile_f32, tile_f32,
            tile_f32, tile_f32,
            tile_f32,
        ],
        compiler_params=pltpu.CompilerParams(
            dimension_semantics=("parallel", "arbitrary"), vmem_limit_bytes=VMEM_LIMIT),
    )(_block_triangular(HG_SUBTILE, False), _block_triangular(HG_SUBTILE, True),
      hg16, hg32, hg16, hg16, hg16, hg32, hg16, hg16)


def _out_ffn_kernel(x_ref, o1_ref, o2_ref, o3_ref, s1_ref, s2_ref, s3_ref, hf_ref, hb_ref, gs_ref,
                    expand_ref, gh_ref, wout_ref, gpm_ref, gpf_ref, gpo_ref, wg_ref, wu_ref, wd_ref,
                    y_ref):
    def mixer_output(rows):
        lses = (s1_ref[rows, :], s2_ref[rows, :], s3_ref[rows, :])
        top = jnp.maximum(jnp.maximum(lses[0], lses[1]), lses[2])
        es = [jnp.exp2(s - top) for s in lses]
        den = es[0] + es[1] + es[2]
        att = None
        for e, o_ref in zip(es, (o1_ref, o2_ref, o3_ref)):
            w = jnp.dot((e / den).astype(BF16), expand_ref[...], preferred_element_type=F32)
            term = w * o_ref[rows, :].astype(F32)
            att = term if att is None else att + term
        o = hf_ref[rows, :].astype(F32) + hb_ref[rows, :].astype(F32)
        hg = jnp.concatenate(
            [_rms(o[:, h * HG_DK:(h + 1) * HG_DK]) * gh_ref[...] for h in range(HG_HEADS)], axis=1)
        hg = hg * gs_ref[rows, :].astype(F32)
        return jnp.concatenate([att, hg], axis=1).astype(BF16)

    tm = x_ref.shape[0]
    halves = [slice(i * (tm // 2), (i + 1) * (tm // 2)) for i in range(2)]
    dot = functools.partial(jnp.dot, preferred_element_type=F32)
    mix_in = [mixer_output(r) for r in halves]
    mix = [dot(m, wout_ref[...]) for m in mix_in]
    x1 = [x_ref[r, :] + _rms(m) * gpm_ref[...] for r, m in zip(halves, mix)]
    h2 = [(_rms(t) * gpf_ref[...]).astype(BF16) for t in x1]
    gate = [dot(t, wg_ref[...]) for t in h2]
    up = [dot(t, wu_ref[...]) for t in h2]
    act = [(_silu(g) * u).astype(BF16) for g, u in zip(gate, up)]
    ff = [dot(t, wd_ref[...]) for t in act]
    for r, t, f in zip(halves, x1, ff):
        y_ref[r, :] = t + _rms(f) * gpo_ref[...]


def _out_ffn(x, att_o, att_s, hf, hb, gs, g_hnorm, w_out, g_pm, g_pf, g_po, w_gate, w_up, w_down, tm):
    batch, seq_len, _ = x.shape
    col_head = np.arange(ATT_WIDTH) // ATT_HEAD_DIM
    expand = np.zeros((LANES, ATT_WIDTH), np.float32)
    expand[_stat_lane(col_head), np.arange(ATT_WIDTH)] = 1.0
    expand = jnp.asarray(expand, dtype=BF16)
    const = lambda b, i: (0, 0)
    tile = lambda w: pl.BlockSpec((None, tm, w), lambda b, i: (b, i, 0))
    whole = lambda a: pl.BlockSpec(a.shape, const, pipeline_mode=pl.Buffered(1))
    consts = (expand, g_hnorm, w_out, g_pm, g_pf, g_po, w_gate, w_up, w_down)
    return pl.pallas_call(
        _out_ffn_kernel,
        grid=(batch, seq_len // tm),
        in_specs=[tile(D_MODEL)] + [tile(ATT_WIDTH)] * 3 + [tile(LANES)] * 3 + [tile(HG_WIDTH)] * 2
                 + [pl.BlockSpec((None, tm, GROUP_W), lambda b, i: (b, i, HG16_FIELDS.index("gs")))]
                 + [whole(a) for a in consts],
        out_specs=tile(D_MODEL),
        out_shape=jax.ShapeDtypeStruct((batch, seq_len, D_MODEL), F32),
        compiler_params=pltpu.CompilerParams(
            dimension_semantics=("parallel", "parallel"), vmem_limit_bytes=VMEM_LIMIT),
    )(x, *att_o, *att_s, hf, hb, gs, *consts)


def _layer(x, tables, w_in, w_out, lb_fwd, lb_bwd, g_hnorm, g_pre_mix, g_post_mix, g_pre_ffn, g_post_ffn,
           w_gate, w_up, w_down, *, tm_in=IN_PROJ_ROWS, th=HG_TILE_ROWS, tm_out=OUT_FFN_ROWS):
    batch, seq_len, _ = x.shape
    x2 = x.reshape(batch * seq_len, D_MODEL)
    *qkv, hg16, hg32 = _in_proj(x2, g_pre_mix, w_in, tables, lb_fwd, lb_bwd, seq_len, tm_in)
    att = [_attn_branch(t, batch, seq_len, dil) for t, (_, dil) in zip(qkv, DILATED_BRANCHES)]
    hf, hb = _hgrn(hg16, hg32, batch, seq_len, th)
    return _out_ffn(x, [o for o, _ in att], [s for _, s in att], hf, hb,
                    hg16.reshape(batch, seq_len, len(HG16_FIELDS) * GROUP_W), g_hnorm, w_out,
                    g_post_mix, g_pre_ffn, g_post_ffn, w_gate, w_up, w_down, tm_out)


def kernel(x_prompt, x_sample, w_in, w_out, lb_fwd, lb_bwd, g_hgrn_norm, g_pre_mix, g_post_mix,
           g_pre_ffn, g_post_ffn, w_gate, w_up, w_down):
    assert w_in.shape[0] == 1, "one layer"
    assert all(w // (2 * d) == ATT_HALF for w, d in DILATED_BRANCHES)
    params = (w_in[0].astype(BF16), w_out[0].astype(BF16), lb_fwd, lb_bwd, g_hgrn_norm,
              g_pre_mix, g_post_mix, g_pre_ffn, g_post_ffn,
              w_gate[0].astype(BF16), w_up[0].astype(BF16), w_down[0].astype(BF16))
    tables = _rope_tables(max(x_prompt.shape[1], x_sample.shape[1]), IN_PROJ_ROWS)
    return _layer(x_prompt, tables, *params), _layer(x_sample, tables, *params)
```
